```python
import jax, jax.numpy as jnp
from jax import lax
import numpy as np

D_MODEL = 1024
BATCH = 16
SEQ = 4096
DEPTH = 1
DEC_BATCH = 2
DEC_SEQ = 8192
PAST_LEN = 128

GRID_W = 64
D_MIX = D_MODEL
ATT_HEADS = 8
HEAD_DIM = 64
D_ATT = ATT_HEADS * HEAD_DIM
D_LRU = D_MIX - D_ATT
LRU_HEADS = 8
LRU_BLOCK = D_LRU // LRU_HEADS
NA_ROWS = 8
NA_COLS = 16
LRU_CONV = 4
LRU_C = 8.0
FFN_CONV = 3
D_FF = 2816
D_IN = 3 * D_ATT + 2 * D_LRU
N_MOD = 6
NORM_EPS = 1e-6

kernel_name = 'hybrid_natten_rglru_encoder'


def rms_norm(x, g):
    xf = x.astype(jnp.float32)
    y = xf * lax.rsqrt(jnp.mean(xf * xf, axis=-1, keepdims=True) + NORM_EPS)
    return (y * g.astype(jnp.float32)).astype(x.dtype)


def depthwise_conv(x, w, b):
    k = w.shape[0]
    lo = (k - 1) // 2
    y = lax.conv_general_dilated(
        x, w.astype(x.dtype)[:, None, :], window_strides=(1,), padding=[(lo, k - 1 - lo)],
        dimension_numbers=('NWC', 'WIO', 'NWC'), feature_group_count=x.shape[-1])
    return y + b.astype(x.dtype)


def neighbourhood_attention(q, k, v, rpb):
    bsz, length, heads, dh = q.shape
    rows = length // GRID_W
    wr = min(NA_ROWS, rows)
    qg = q.reshape(bsz, rows, GRID_W, heads, dh)
    kg = k.reshape(bsz, rows, GRID_W, heads, dh)
    vg = v.reshape(bsz, rows, GRID_W, heads, dh)
    cols = jnp.arange(GRID_W)
    col_start = jnp.clip(cols - NA_COLS // 2, 0, GRID_W - NA_COLS)
    col_idx = col_start[:, None] + jnp.arange(NA_COLS)[None, :]
    dc = col_idx - cols[:, None] + (NA_COLS - 1)
    scale = HEAD_DIM ** -0.5

    def row_block(r):
        rs = jnp.clip(r - wr // 2, 0, rows - wr)
        qr = lax.dynamic_index_in_dim(qg, r, axis=1, keepdims=False)
        kb = lax.dynamic_slice_in_dim(kg, rs, wr, axis=1)
        vb = lax.dynamic_slice_in_dim(vg, rs, wr, axis=1)
        kw = kb[:, :, col_idx]
        vw = vb[:, :, col_idx]
        dr = rs + jnp.arange(wr) - r + (NA_ROWS - 1)
        bias = rpb[:, dr[None, :, None], dc[:, None, :]]
        s = jnp.einsum('bchd,brcjhd->bhcrj', qr, kw).astype(jnp.float32) * scale
        s = s + bias.astype(jnp.float32)[None]
        p = jax.nn.softmax(s, axis=(-2, -1))
        return jnp.einsum('bhcrj,brcjhd->bchd', p.astype(v.dtype), vw)

    out = lax.map(row_block, jnp.arange(rows))
    return jnp.transpose(out, (1, 0, 2, 3, 4)).reshape(bsz, length, heads * dh)


def _lin_combine(e1, e2):
    a1, b1 = e1
    a2, b2 = e2
    return a1 * a2, a2 * b1 + b2


def rg_lru_direction(x, w_r, b_r, w_i, b_i, lam, reverse):
    bsz, t, _ = x.shape
    xb = x.reshape(bsz, t, LRU_HEADS, LRU_BLOCK)
    r = jax.nn.sigmoid((jnp.einsum('bthd,hde->bthe', xb, w_r).reshape(bsz, t, D_LRU) + b_r).astype(jnp.float32))
    i = jax.nn.sigmoid((jnp.einsum('bthd,hde->bthe', xb, w_i).reshape(bsz, t, D_LRU) + b_i).astype(jnp.float32))
    log_a = -LRU_C * r * jax.nn.softplus(-lam.astype(jnp.float32))
    a = jnp.exp(log_a)
    u = jnp.sqrt(-jnp.expm1(2.0 * log_a)) * (i * x.astype(jnp.float32))
    _, h = lax.associative_scan(_lin_combine, (a, u), reverse=reverse, axis=1)
    return h


def encoder_layer(x, c, ln1_g, ln2_g, w_ada, b_ada, w_in, q_norm_g, k_norm_g, rpb,
                  lru_conv_w, lru_conv_b, w_r_f, b_r_f, w_i_f, b_i_f, lam_f,
                  w_r_b, b_r_b, w_i_b, b_i_b, lam_b, w_out, w_up, ffn_conv_w, ffn_conv_b, w_down):
    bsz, t, _ = x.shape
    mod = (jax.nn.silu(c) @ w_ada + b_ada)[:, None, :]
    sh1, sc1, g1, sh2, sc2, g2 = jnp.split(mod, N_MOD, axis=-1)
    n = rms_norm(x, ln1_g) * (1 + sc1) + sh1
    z = n @ w_in
    q, k, v, xl, gl = jnp.split(z, [D_ATT, 2 * D_ATT, 3 * D_ATT, 3 * D_ATT + D_LRU], axis=-1)
    q = rms_norm(q.reshape(bsz, t, ATT_HEADS, HEAD_DIM), q_norm_g)
    k = rms_norm(k.reshape(bsz, t, ATT_HEADS, HEAD_DIM), k_norm_g)
    v = v.reshape(bsz, t, ATT_HEADS, HEAD_DIM)
    att = neighbourhood_attention(q, k, v, rpb)
    xl = depthwise_conv(xl, lru_conv_w, lru_conv_b)
    h = (rg_lru_direction(xl, w_r_f, b_r_f, w_i_f, b_i_f, lam_f, False)
         + rg_lru_direction(xl, w_r_b, b_r_b, w_i_b, b_i_b, lam_b, True))
    lru = (h * jax.nn.gelu(gl.astype(jnp.float32))).astype(x.dtype)
    mix = jnp.concatenate([att, lru], axis=-1) @ w_out
    x = x + g1 * mix
    n2 = rms_norm(x, ln2_g) * (1 + sc2) + sh2
    up = depthwise_conv(n2 @ w_up, ffn_conv_w, ffn_conv_b)
    ug, uv = jnp.split(up, 2, axis=-1)
    x = x + g2 * ((jax.nn.gelu(ug) * uv) @ w_down)
    return x


def trunk(x, c, params):
    for layer in range(DEPTH):
        x = encoder_layer(x, c, *[p[layer] for p in params])
    return x


def setup_inputs(seed: int = 0) -> dict:
    key = jax.random.key(seed)
    ks = jax.random.split(key, 32)

    def nrm(k, shape, scale):
        return jax.random.normal(k, shape, jnp.float32) * scale

    lam_u = jax.random.uniform(ks[30], (2, DEPTH, D_LRU), jnp.float32, 0.9, 0.999)
    a0 = lam_u ** (1.0 / LRU_C)
    lam0 = jnp.log(a0) - jnp.log1p(-a0)
    gate_s = LRU_BLOCK ** -0.5
    return {
        'x_prompt': nrm(ks[0], (BATCH, SEQ, D_MODEL), 1.0),
        'x_sample': nrm(ks[1], (DEC_BATCH, DEC_SEQ, D_MODEL), 1.0),
        'c_prompt': nrm(ks[2], (BATCH, D_MODEL), 1.0),
        'c_sample': nrm(ks[3], (DEC_BATCH, D_MODEL), 1.0),
        'ln1_g': 1.0 + nrm(ks[4], (DEPTH, D_MODEL), 0.02),
        'ln2_g': 1.0 + nrm(ks[5], (DEPTH, D_MODEL), 0.02),
        'w_ada': nrm(ks[6], (DEPTH, D_MODEL, N_MOD * D_MODEL), 0.5 * D_MODEL ** -0.5),
        'b_ada': nrm(ks[7], (DEPTH, N_MOD * D_MODEL), 0.02),
        'w_in': nrm(ks[8], (DEPTH, D_MODEL, D_IN), D_MODEL ** -0.5),
        'q_norm_g': 1.0 + nrm(ks[9], (DEPTH, HEAD_DIM), 0.02),
        'k_norm_g': 1.0 + nrm(ks[10], (DEPTH, HEAD_DIM), 0.02),
        'rpb': nrm(ks[11], (DEPTH, ATT_HEADS, 2 * NA_ROWS - 1, 2 * NA_COLS - 1), 0.1),
        'lru_conv_w': nrm(ks[12], (DEPTH, LRU_CONV, D_LRU), LRU_CONV ** -0.5),
        'lru_conv_b': nrm(ks[13], (DEPTH, D_LRU), 0.02),
        'w_r_f': nrm(ks[14], (DEPTH, LRU_HEADS, LRU_BLOCK, LRU_BLOCK), gate_s),
        'b_r_f': nrm(ks[15], (DEPTH, D_LRU), 0.02),
        'w_i_f': nrm(ks[16], (DEPTH, LRU_HEADS, LRU_BLOCK, LRU_BLOCK), gate_s),
        'b_i_f': nrm(ks[17], (DEPTH, D_LRU), 0.02),
        'lam_f': lam0[0],
        'w_r_b': nrm(ks[18], (DEPTH, LRU_HEADS, LRU_BLOCK, LRU_BLOCK), gate_s),
        'b_r_b': nrm(ks[19], (DEPTH, D_LRU), 0.02),
        'w_i_b': nrm(ks[20], (DEPTH, LRU_HEADS, LRU_BLOCK, LRU_BLOCK), gate_s),
        'b_i_b': nrm(ks[21], (DEPTH, D_LRU), 0.02),
        'lam_b': lam0[1],
        'w_out': nrm(ks[22], (DEPTH, D_MIX, D_MODEL), D_MIX ** -0.5),
        'w_up': nrm(ks[23], (DEPTH, D_MODEL, 2 * D_FF), D_MODEL ** -0.5),
        'ffn_conv_w': nrm(ks[24], (DEPTH, FFN_CONV, 2 * D_FF), FFN_CONV ** -0.5),
        'ffn_conv_b': nrm(ks[25], (DEPTH, 2 * D_FF), 0.02),
        'w_down': nrm(ks[26], (DEPTH, D_FF, D_MODEL), D_FF ** -0.5),
    }


def reference(x_prompt, x_sample, c_prompt, c_sample, ln1_g, ln2_g, w_ada, b_ada, w_in,
              q_norm_g, k_norm_g, rpb, lru_conv_w, lru_conv_b, w_r_f, b_r_f, w_i_f, b_i_f, lam_f,
              w_r_b, b_r_b, w_i_b, b_i_b, lam_b, w_out, w_up, ffn_conv_w, ffn_conv_b, w_down):
    params = (ln1_g, ln2_g, w_ada, b_ada, w_in, q_norm_g, k_norm_g, rpb,
              lru_conv_w, lru_conv_b, w_r_f, b_r_f, w_i_f, b_i_f, lam_f,
              w_r_b, b_r_b, w_i_b, b_i_b, lam_b, w_out, w_up, ffn_conv_w, ffn_conv_b, w_down)
    y_prompt = trunk(x_prompt, c_prompt, params)
    y_sample = trunk(x_sample, c_sample, params)
    return (y_prompt, y_sample)
```

```python
import functools
import math

import jax
import jax.numpy as jnp
from jax import lax
from jax.experimental import pallas as pl
from jax.experimental.pallas import tpu as pltpu

D_MODEL = 1024
GRID_W = 64
ATT_HEADS = 8
HEAD_DIM = 64
D_ATT = ATT_HEADS * HEAD_DIM
D_LRU = 512
LRU_HEADS = 8
LRU_BLOCK = D_LRU // LRU_HEADS
NA_ROWS = 8
NA_COLS = 16
LRU_CONV = 4
LRU_C = 8.0
FFN_CONV = 3
D_FF = 2816
D_IN = 3 * D_ATT + 2 * D_LRU
N_MOD = 6
NORM_EPS = 1e-6

LANES = 128
SUBLANES = 8
N_GROUPS = D_ATT // LANES
WIN_KEYS = NA_ROWS * GRID_W
FF_CHUNK = 256
N_FF_CHUNKS = D_FF // FF_CHUNK
HALO = 16
MASK_VALUE = -1e30
VMEM_LIMIT = 56 * 1024 * 1024

BF16 = jnp.bfloat16
F32 = jnp.float32


def _gelu_tanh(x):
    return 0.5 * x * (1.0 + jnp.tanh(math.sqrt(2.0 / math.pi) * (x + 0.044715 * (x * x * x))))


def _sigmoid(x):
    return 0.5 * jnp.tanh(0.5 * x) + 0.5


def _const_spec(shape):
    zeros = (0,) * len(shape)
    return pl.BlockSpec(shape, lambda *_: zeros, pipeline_mode=pl.Buffered(1))


def _mod_kernel(c_ref, w_ref, b_ref, o_ref):
    c = c_ref[...]
    s = (c * _sigmoid(c)).astype(BF16)
    o_ref[...] = jnp.dot(s, w_ref[...].astype(BF16), preferred_element_type=F32) + b_ref[...]


def _modulation(c_all, w_ada, b_ada):
    bp = c_all.shape[0]
    return pl.pallas_call(
        _mod_kernel,
        grid=(N_MOD,),
        in_specs=[
            pl.BlockSpec((bp, D_MODEL), lambda j: (0, 0)),
            pl.BlockSpec((D_MODEL, D_MODEL), lambda j: (0, j)),
            pl.BlockSpec((1, D_MODEL), lambda j: (0, j)),
        ],
        out_specs=pl.BlockSpec((bp, D_MODEL), lambda j: (0, j)),
        out_shape=jax.ShapeDtypeStruct((bp, N_MOD * D_MODEL), F32),
        compiler_params=pltpu.CompilerParams(dimension_semantics=("arbitrary",), vmem_limit_bytes=VMEM_LIMIT),
        name="modulation",
    )(c_all, w_ada, b_ada.reshape(1, -1))


def _inproj_kernel(x_ref, mod_ref, ln_ref, w_ref, gmat_ref, qkg_ref,
                   q_ref, k_ref, v_ref, xl_ref, gl_ref):
    x = x_ref[...]
    ms = jnp.mean(x * x, axis=-1, keepdims=True)
    y = x * lax.rsqrt(ms + NORM_EPS) * ln_ref[...]
    n = (y * (1.0 + mod_ref[1:2, :]) + mod_ref[0:1, :]).astype(BF16)

    def head_norm(z, gain):
        ss = jnp.dot((z * z).astype(BF16), gmat_ref[...], preferred_element_type=F32)
        return z * lax.rsqrt(ss * (1.0 / HEAD_DIM) + NORM_EPS) * gain

    outs = (q_ref, k_ref, v_ref, xl_ref, gl_ref)
    for idx, o_ref in enumerate(outs):
        z = jnp.dot(n, w_ref[:, idx * D_ATT:(idx + 1) * D_ATT], preferred_element_type=F32)
        if idx < 2:
            z = head_norm(z, qkg_ref[idx:idx + 1, :])
        for p in range(N_GROUPS):
            o_ref[p] = z[:, p * LANES:(p + 1) * LANES].astype(BF16)


def _in_projection(x, mod, ln1_g, w_in_bf, gmat, qkg, tm):
    bsz, t, _ = x.shape
    grouped = jax.ShapeDtypeStruct((bsz, N_GROUPS, t, LANES), BF16)
    out_spec = pl.BlockSpec((None, N_GROUPS, tm, LANES), lambda b, i: (b, 0, i, 0))
    return pl.pallas_call(
        _inproj_kernel,
        grid=(bsz, t // tm),
        in_specs=[
            pl.BlockSpec((None, tm, D_MODEL), lambda b, i: (b, i, 0)),
            pl.BlockSpec((None, SUBLANES, D_MODEL), lambda b, i: (b, 0, 0)),
            _const_spec((1, D_MODEL)),
            _const_spec((D_MODEL, D_IN)),
            _const_spec((D_ATT, D_ATT)),
            _const_spec((SUBLANES, D_ATT)),
        ],
        out_specs=[out_spec] * 5,
        out_shape=[grouped] * 5,
        compiler_params=pltpu.CompilerParams(
            dimension_semantics=("parallel", "parallel"), vmem_limit_bytes=VMEM_LIMIT),
        name="in_projection",
    )(x, mod, ln1_g, w_in_bf, gmat, qkg)


def _attn_kernel(q_ref, k_ref, v_ref, bias_ref, o_ref, *, rows):
    lane = lax.broadcasted_iota(jnp.int32, (GRID_W, LANES), 1)
    low = lane < HEAD_DIM

    def row_step(r, carry):
        rs = jnp.clip(r - NA_ROWS // 2, 0, rows - NA_ROWS)
        off = r - rs
        q0 = pl.multiple_of(r * GRID_W, GRID_W)
        k0 = pl.multiple_of(rs * GRID_W, GRID_W)
        q2 = q_ref[pl.ds(q0, GRID_W), :].astype(F32)
        qq = jnp.concatenate([jnp.where(low, q2, 0.0), jnp.where(low, 0.0, q2)], axis=0).astype(BF16)
        k2 = k_ref[pl.ds(k0, WIN_KEYS), :]
        v2 = v_ref[pl.ds(k0, WIN_KEYS), :]
        s = lax.dot_general(qq, k2, (((1,), (1,)), ((), ())), preferred_element_type=F32)
        s = s + bias_ref[off]
        m = jnp.max(s, axis=-1, keepdims=True)
        e = jnp.exp(s - m)
        l = jnp.sum(e, axis=-1, keepdims=True)
        pv = jnp.dot(e.astype(BF16), v2, preferred_element_type=F32)
        pv = pv * (1.0 / l)
        out = jnp.where(low, pv[:GRID_W], pv[GRID_W:])
        o_ref[pl.ds(q0, GRID_W), :] = out.astype(BF16)
        return carry

    lax.fori_loop(0, rows, row_step, 0)


def _attention(q, k, v, bias):
    bsz, _, t, _ = q.shape
    rows = t // GRID_W
    seq_spec = pl.BlockSpec((None, None, t, LANES), lambda p, b: (b, p, 0, 0))
    return pl.pallas_call(
        functools.partial(_attn_kernel, rows=rows),
        grid=(N_GROUPS, bsz),
        in_specs=[seq_spec, seq_spec, seq_spec,
                  pl.BlockSpec((None, NA_ROWS, 2 * GRID_W, WIN_KEYS), lambda p, b: (p, 0, 0, 0))],
        out_specs=seq_spec,
        out_shape=jax.ShapeDtypeStruct(q.shape, BF16),
        compiler_params=pltpu.CompilerParams(
            dimension_semantics=("parallel", "parallel"), vmem_limit_bytes=VMEM_LIMIT),
        name="nbr_attention",
    )(q, k, v, bias)


def _attention_bias(rpb):
    cols = jnp.arange(GRID_W)
    col_start = jnp.clip(cols - NA_COLS // 2, 0, GRID_W - NA_COLS)
    key_col = jnp.arange(GRID_W)
    in_win = (key_col[None, :] >= col_start[:, None]) & (key_col[None, :] < col_start[:, None] + NA_COLS)
    dc = jnp.clip(key_col[None, :] - cols[:, None] + (NA_COLS - 1), 0, 2 * NA_COLS - 2)
    off = jnp.arange(NA_ROWS)
    win_row = jnp.arange(NA_ROWS)
    dr = win_row[None, :] - off[:, None] + (NA_ROWS - 1)
    b = rpb.astype(F32)[:, dr[:, :, None, None], dc[None, None, :, :]]
    b = jnp.where(in_win[None, None, None], b, MASK_VALUE)
    b = jnp.transpose(b, (0, 1, 3, 2, 4)).reshape(ATT_HEADS, NA_ROWS, GRID_W, WIN_KEYS)
    b = b.reshape(N_GROUPS, 2, NA_ROWS, GRID_W, WIN_KEYS)
    return jnp.transpose(b, (0, 2, 1, 3, 4)).reshape(N_GROUPS, NA_ROWS, 2 * GRID_W, WIN_KEYS)


def _lru_kernel(xl_ref, gl_ref, cw_ref, wf_ref, wb_ref, par_ref, o_ref, xp_s, xc_s, hb_s, *, t, tc):
    n_chunks = t // tc
    n_vregs = tc // SUBLANES
    zeros8 = jnp.zeros((SUBLANES, LANES), F32)
    xp_s[0:SUBLANES, :] = zeros8
    xp_s[SUBLANES + t:, :] = zeros8
    xp_s[SUBLANES:SUBLANES + t, :] = xl_ref[...].astype(F32)

    sub = lax.broadcasted_iota(jnp.int32, (tc, LANES), 0) % SUBLANES

    def softplus(z):
        return jnp.maximum(z, 0.0) + jnp.log(1.0 + jnp.exp(-jnp.abs(z)))

    def gates(xc, w_ref, row0):
        g = jnp.dot(xc.astype(BF16), w_ref[...], preferred_element_type=F32)
        r = _sigmoid(g[:, :LANES] + par_ref[row0:row0 + 1, :])
        i = _sigmoid(g[:, LANES:] + par_ref[row0 + 1:row0 + 2, :])
        log_a = (-LRU_C * softplus(-par_ref[row0 + 2:row0 + 3, :])) * r
        a = jnp.exp(log_a)
        th = jnp.tanh(log_a)
        u = jnp.sqrt(-2.0 * th / (1.0 - th)) * (i * xc)
        return a, u

    def local_scan(a, u, reverse):
        for d in (1, 2, 4):
            if reverse:
                shift, ok = tc - d, sub < SUBLANES - d
            else:
                shift, ok = d, sub >= d
            a_sh = jnp.where(ok, pltpu.roll(a, shift, 0), 1.0)
            u_sh = jnp.where(ok, pltpu.roll(u, shift, 0), 0.0)
            u = u + a * u_sh
            a = a * a_sh
        return a, u

    def chain(p, u, carry, reverse):
        src = 0 if reverse else SUBLANES - 1
        order = range(n_vregs - 1, -1, -1) if reverse else range(n_vregs)
        hs = [None] * n_vregs
        for j in order:
            pj = p[j * SUBLANES:(j + 1) * SUBLANES]
            uj = u[j * SUBLANES:(j + 1) * SUBLANES]
            hs[j] = uj + pj * jnp.broadcast_to(carry[src:src + 1, :], (SUBLANES, LANES))
            carry = uj + pj * carry
        return jnp.concatenate(hs, axis=0), carry

    def conv(c):
        t0 = pl.multiple_of(c * tc, tc)
        xe = xp_s[pl.ds(t0, tc + 2 * SUBLANES), :]
        n = tc + 2 * SUBLANES
        acc = cw_ref[1:2, :] * xe[SUBLANES:SUBLANES + tc] + cw_ref[4:5, :]
        for tap, shift in ((0, 1), (2, n - 1), (3, n - 2)):
            acc = acc + cw_ref[tap:tap + 1, :] * pltpu.roll(xe, shift, 0)[SUBLANES:SUBLANES + tc]
        return t0, acc

    def bwd_step(idx, carry):
        t0, xc = conv(n_chunks - 1 - idx)
        xc_s[pl.ds(t0, tc), :] = xc
        a, u = gates(xc, wb_ref, 3)
        p, u = local_scan(a, u, True)
        h, carry = chain(p, u, carry, True)
        hb_s[pl.ds(t0, tc), :] = h
        return carry

    lax.fori_loop(0, n_chunks, bwd_step, zeros8)

    def fwd_step(c, carry):
        t0 = pl.multiple_of(c * tc, tc)
        xc = xc_s[pl.ds(t0, tc), :]
        a, u = gates(xc, wf_ref, 0)
        p, u = local_scan(a, u, False)
        h, carry = chain(p, u, carry, False)
        h = h + hb_s[pl.ds(t0, tc), :]
        o_ref[pl.ds(t0, tc), :] = (h * _gelu_tanh(gl_ref[pl.ds(t0, tc), :].astype(F32))).astype(BF16)
        return carry

    lax.fori_loop(0, n_chunks, fwd_step, zeros8)


def _rg_lru(xl, gl, cw, wf, wb, par, tc):
    bsz, _, t, _ = xl.shape
    seq_spec = pl.BlockSpec((None, None, t, LANES), lambda b, g: (b, g, 0, 0))

    def group_spec(shape):
        return pl.BlockSpec((None,) + shape, lambda b, g: (g,) + (0,) * len(shape))

    return pl.pallas_call(
        functools.partial(_lru_kernel, t=t, tc=tc),
        grid=(bsz, N_GROUPS),
        in_specs=[seq_spec, seq_spec, group_spec((SUBLANES, LANES)), group_spec((LANES, 2 * LANES)),
                  group_spec((LANES, 2 * LANES)), group_spec((SUBLANES, LANES))],
        out_specs=seq_spec,
        out_shape=jax.ShapeDtypeStruct(xl.shape, BF16),
        scratch_shapes=[pltpu.VMEM((t + 2 * SUBLANES, LANES), F32),
                        pltpu.VMEM((t, LANES), F32),
                        pltpu.VMEM((t, LANES), F32)],
        compiler_params=pltpu.CompilerParams(
            dimension_semantics=("parallel", "parallel"), vmem_limit_bytes=VMEM_LIMIT),
        name="rg_lru",
    )(xl, gl, cw, wf, wb, par)


def _block_diag_pairs(w):
    w = w.reshape(N_GROUPS, 2, LRU_BLOCK, LRU_BLOCK)
    z = jnp.zeros((N_GROUPS, LRU_BLOCK, LRU_BLOCK), w.dtype)
    top = jnp.concatenate([w[:, 0], z], axis=-1)
    bot = jnp.concatenate([z, w[:, 1]], axis=-1)
    return jnp.concatenate([top, bot], axis=-2)


def _mlp_kernel(x_ref, xp_ref, xn_ref, a_ref, ap_ref, an_ref, l_ref, lp_ref, ln_ref, mod_ref, ln2_ref,
                wout_ref, wup_ref, cw_ref, wdn_ref, o_ref, al_s, x1_s, n2_s, acc_s, *, tm, t):
    i = pl.program_id(1)
    n_ext = tm + 2 * HALO
    for src, lane0 in (((ap_ref, a_ref, an_ref), 0), ((lp_ref, l_ref, ln_ref), D_ATT)):
        prev_ref, main_ref, next_ref = src
        for p in range(N_GROUPS):
            lanes = slice(lane0 + p * LANES, lane0 + (p + 1) * LANES)
            al_s[0:HALO, lanes] = prev_ref[p]
            al_s[HALO:HALO + tm, lanes] = main_ref[p]
            al_s[HALO + tm:, lanes] = next_ref[p]
    x1_s[0:HALO, :] = xp_ref[...]
    x1_s[HALO:HALO + tm, :] = x_ref[...]
    x1_s[HALO + tm:, :] = xn_ref[...]

    mix = jnp.dot(al_s[...], wout_ref[...], preferred_element_type=F32)
    x1 = x1_s[...] + mod_ref[2:3, :] * mix
    x1_s[...] = x1
    ms = jnp.mean(x1 * x1, axis=-1, keepdims=True)
    n2 = x1 * lax.rsqrt(ms + NORM_EPS) * ln2_ref[...] * (1.0 + mod_ref[4:5, :]) + mod_ref[3:4, :]
    tok = i * tm - HALO + lax.broadcasted_iota(jnp.int32, (n_ext, 1), 0)
    n2_s[...] = jnp.where((tok >= 0) & (tok < t), n2, 0.0).astype(BF16)
    acc_s[...] = jnp.zeros_like(acc_s)

    def conv(u, cw):
        c = cw[1:2, :] * u[HALO:HALO + tm] + cw[3:4, :]
        c = c + cw[0:1, :] * pltpu.roll(u, 1, 0)[HALO:HALO + tm]
        return c + cw[2:3, :] * pltpu.roll(u, n_ext - 1, 0)[HALO:HALO + tm]

    def chunk(j, carry):
        n2 = n2_s[...]
        ug = conv(jnp.dot(n2, wup_ref[j], preferred_element_type=F32), cw_ref[j])
        uv = conv(jnp.dot(n2, wup_ref[N_FF_CHUNKS + j], preferred_element_type=F32), cw_ref[N_FF_CHUNKS + j])
        act = (_gelu_tanh(ug) * uv).astype(BF16)
        acc_s[...] += jnp.dot(act, wdn_ref[j], preferred_element_type=F32)
        return carry

    lax.fori_loop(0, N_FF_CHUNKS, chunk, 0)
    o_ref[...] = x1_s[HALO:HALO + tm, :] + mod_ref[5:6, :] * acc_s[...]


def _out_mlp(x, att, lru, mod, ln2_g, w_out_bf, w_up_c, cw_c, w_dn_c, tm):
    bsz, t, _ = x.shape
    per_halo = tm // HALO
    n_halo = t // HALO

    def main_map(b, i):
        return (b, i, 0)

    def prev_map(b, i):
        return (b, jnp.maximum(i * per_halo - 1, 0), 0)

    def next_map(b, i):
        return (b, jnp.minimum((i + 1) * per_halo, n_halo - 1), 0)

    def grouped(rows, imap):
        return pl.BlockSpec((None, N_GROUPS, rows, LANES), lambda b, i: (imap(b, i)[0], 0, imap(b, i)[1], 0))

    n_ext = tm + 2 * HALO
    return pl.pallas_call(
        functools.partial(_mlp_kernel, tm=tm, t=t),
        grid=(bsz, t // tm),
        in_specs=[
            pl.BlockSpec((None, tm, D_MODEL), main_map),
            pl.BlockSpec((None, HALO, D_MODEL), prev_map),
            pl.BlockSpec((None, HALO, D_MODEL), next_map),
            grouped(tm, main_map), grouped(HALO, prev_map), grouped(HALO, next_map),
            grouped(tm, main_map), grouped(HALO, prev_map), grouped(HALO, next_map),
            pl.BlockSpec((None, SUBLANES, D_MODEL), lambda b, i: (b, 0, 0)),
            _const_spec((1, D_MODEL)),
            _const_spec((D_MODEL, D_MODEL)),
            _const_spec((2 * N_FF_CHUNKS, D_MODEL, FF_CHUNK)),
            _const_spec((2 * N_FF_CHUNKS, SUBLANES, FF_CHUNK)),
            _const_spec((N_FF_CHUNKS, FF_CHUNK, D_MODEL)),
        ],
        out_specs=pl.BlockSpec((None, tm, D_MODEL), main_map),
        out_shape=jax.ShapeDtypeStruct(x.shape, F32),
        scratch_shapes=[pltpu.VMEM((n_ext, D_MODEL), BF16),
                        pltpu.VMEM((n_ext, D_MODEL), F32),
                        pltpu.VMEM((n_ext, D_MODEL), BF16),
                        pltpu.VMEM((tm, D_MODEL), F32)],
        compiler_params=pltpu.CompilerParams(
            dimension_semantics=("parallel", "parallel"), vmem_limit_bytes=VMEM_LIMIT),
        name="out_mlp",
    )(x, x, x, att, att, att, lru, lru, lru, mod, ln2_g, w_out_bf, w_up_c, cw_c, w_dn_c)


def _pad_rows(a, rows):
    return jnp.concatenate([a, jnp.zeros((rows - a.shape[0],) + a.shape[1:], a.dtype)], axis=0)


def _prepare_params(ln1_g, ln2_g, w_in, q_norm_g, k_norm_g, rpb, lru_conv_w, lru_conv_b,
                    w_r_f, b_r_f, w_i_f, b_i_f, lam_f, w_r_b, b_r_b, w_i_b, b_i_b, lam_b,
                    w_out, w_up, ffn_conv_w, ffn_conv_b, w_down):
    head = jnp.arange(D_ATT) // HEAD_DIM
    gmat = (head[:, None] == head[None, :]).astype(BF16)
    scale = HEAD_DIM ** -0.5
    qkg = _pad_rows(jnp.stack([jnp.tile(q_norm_g.astype(F32), ATT_HEADS) * scale,
                               jnp.tile(k_norm_g.astype(F32), ATT_HEADS)]), SUBLANES)
    lru_cw = _pad_rows(jnp.concatenate([lru_conv_w, lru_conv_b[None]], axis=0), SUBLANES)
    lru_cw = jnp.transpose(lru_cw.reshape(SUBLANES, N_GROUPS, LANES), (1, 0, 2))
    wf = jnp.concatenate([_block_diag_pairs(w_r_f), _block_diag_pairs(w_i_f)], axis=-1).astype(BF16)
    wb = jnp.concatenate([_block_diag_pairs(w_r_b), _block_diag_pairs(w_i_b)], axis=-1).astype(BF16)
    par = _pad_rows(jnp.stack([b_r_f, b_i_f, lam_f, b_r_b, b_i_b, lam_b]), SUBLANES)
    par = jnp.transpose(par.reshape(SUBLANES, N_GROUPS, LANES), (1, 0, 2))
    w_up_c = jnp.transpose(w_up.astype(BF16).reshape(D_MODEL, 2 * N_FF_CHUNKS, FF_CHUNK), (1, 0, 2))
    ffn_cw = _pad_rows(jnp.concatenate([ffn_conv_w, ffn_conv_b[None]], axis=0), SUBLANES)
    ffn_cw = jnp.transpose(ffn_cw.reshape(SUBLANES, 2 * N_FF_CHUNKS, FF_CHUNK), (1, 0, 2))
    return dict(
        ln1_g=ln1_g.reshape(1, D_MODEL), ln2_g=ln2_g.reshape(1, D_MODEL),
        w_in=w_in.astype(BF16), gmat=gmat, qkg=qkg, bias=_attention_bias(rpb),
        lru_cw=lru_cw, wf=wf, wb=wb, par=par,
        w_out=w_out.astype(BF16), w_up_c=w_up_c, ffn_cw=ffn_cw,
        w_dn_c=w_down.astype(BF16).reshape(N_FF_CHUNKS, FF_CHUNK, D_MODEL))


def _encoder_layer(x, mod, p, tm, tc):
    q, k, v, xl, gl = _in_projection(x, mod, p["ln1_g"], p["w_in"], p["gmat"], p["qkg"], tm)
    att = _attention(q, k, v, p["bias"])
    lru = _rg_lru(xl, gl, p["lru_cw"], p["wf"], p["wb"], p["par"], tc)
    return _out_mlp(x, att, lru, mod, p["ln2_g"], p["w_out"], p["w_up_c"], p["ffn_cw"], p["w_dn_c"], tm)


def _forward(x_prompt, x_sample, c_prompt, c_sample, w_ada, b_ada, layer_params, tm=512, tc=512):
    nb_p, nb_s = c_prompt.shape[0], c_sample.shape[0]
    depth = w_ada.shape[0]
    for layer in range(depth):
        c_all = jnp.concatenate([c_prompt, c_sample], axis=0)
        c_all = _pad_rows(c_all, -(-c_all.shape[0] // SUBLANES) * SUBLANES)
        mod = _modulation(c_all, w_ada[layer], b_ada[layer])
        mod = mod.reshape(mod.shape[0], N_MOD, D_MODEL)
        mod = jnp.concatenate([mod, jnp.zeros((mod.shape[0], SUBLANES - N_MOD, D_MODEL), F32)], axis=1)
        p = _prepare_params(*[w[layer] for w in layer_params])
        x_prompt = _encoder_layer(x_prompt, mod[:nb_p], p, tm, tc)
        x_sample = _encoder_layer(x_sample, mod[nb_p:nb_p + nb_s], p, tm, tc)
    return x_prompt, x_sample


def kernel(x_prompt, x_sample, c_prompt, c_sample, ln1_g, ln2_g, w_ada, b_ada, w_in, q_norm_g, k_norm_g, rpb,
           lru_conv_w, lru_conv_b, w_r_f, b_r_f, w_i_f, b_i_f, lam_f, w_r_b, b_r_b, w_i_b, b_i_b, lam_b,
           w_out, w_up, ffn_conv_w, ffn_conv_b, w_down):
    layer_params = (ln1_g, ln2_g, w_in, q_norm_g, k_norm_g, rpb, lru_conv_w, lru_conv_b,
                    w_r_f, b_r_f, w_i_f, b_i_f, lam_f, w_r_b, b_r_b, w_i_b, b_i_b, lam_b,
                    w_out, w_up, ffn_conv_w, ffn_conv_b, w_down)
    return _forward(x_prompt, x_sample, c_prompt, c_sample, w_ada, b_ada, layer_params)
```

```python
import functools
import math

import numpy as np

import jax
import jax.numpy as jnp
from jax import lax
from jax.experimental import pallas as pl
from jax.experimental.pallas import tpu as pltpu

D_MODEL = 1024
GRID_W = 64
ATT_HEADS = 8
HEAD_DIM = 64
D_ATT = ATT_HEADS * HEAD_DIM
D_LRU = 512
LRU_HEADS = 8
LRU_BLOCK = D_LRU // LRU_HEADS
NA_ROWS = 8
NA_COLS = 16
LRU_CONV = 4
LRU_C = 8.0
FFN_CONV = 3
D_FF = 2816
D_IN = 3 * D_ATT + 2 * D_LRU
N_MOD = 6
NORM_EPS = 1e-6

LANES = 128
SUBLANES = 8
N_GROUPS = D_ATT // LANES
WIN_KEYS = NA_ROWS * GRID_W
FF_CHUNK = 256
N_FF_CHUNKS = D_FF // FF_CHUNK
HALO = 16
ROW_UNROLL = 8
LRU_STEPS = 64
LRU_SEG_PAD = 8
MASK_VALUE = -1e30
LOG2_E = math.log2(math.e)
VMEM_LIMIT = 56 * 1024 * 1024

BF16 = jnp.bfloat16
F32 = jnp.float32


def _gelu_tanh(x):
    return 0.5 * x * (1.0 + jnp.tanh(math.sqrt(2.0 / math.pi) * (x + 0.044715 * (x * x * x))))


def _sigmoid(x):
    return 0.5 * jnp.tanh(0.5 * x) + 0.5


def _const_spec(shape):
    zeros = (0,) * len(shape)
    return pl.BlockSpec(shape, lambda *_: zeros, pipeline_mode=pl.Buffered(1))


def _mod_kernel(c_ref, w_ref, b_ref, o_ref):
    c = c_ref[...]
    s = (c * _sigmoid(c)).astype(BF16)
    o_ref[...] = jnp.dot(s, w_ref[...].astype(BF16), preferred_element_type=F32) + b_ref[...]


def _modulation(c_all, w_ada, b_ada):
    bp = c_all.shape[0]
    return pl.pallas_call(
        _mod_kernel,
        grid=(N_MOD,),
        in_specs=[
            pl.BlockSpec((bp, D_MODEL), lambda j: (0, 0)),
            pl.BlockSpec((D_MODEL, D_MODEL), lambda j: (0, j)),
            pl.BlockSpec((1, D_MODEL), lambda j: (0, j)),
        ],
        out_specs=pl.BlockSpec((bp, D_MODEL), lambda j: (0, j)),
        out_shape=jax.ShapeDtypeStruct((bp, N_MOD * D_MODEL), F32),
        compiler_params=pltpu.CompilerParams(dimension_semantics=("arbitrary",), vmem_limit_bytes=VMEM_LIMIT),
        name="modulation",
    )(c_all, w_ada, b_ada.reshape(1, -1))


def _inproj_kernel(x_ref, mod_ref, ln_ref, w_ref, gmat_ref, qkg_ref,
                   q_ref, k_ref, v_ref, xl_ref, gl_ref):
    x = x_ref[...]
    ms = jnp.mean(x * x, axis=-1, keepdims=True)
    y = x * lax.rsqrt(ms + NORM_EPS) * ln_ref[...]
    n = (y * (1.0 + mod_ref[1:2, :]) + mod_ref[0:1, :]).astype(BF16)

    def head_norm(z, gain):
        ss = jnp.dot((z * z).astype(BF16), gmat_ref[...], preferred_element_type=F32)
        return z * lax.rsqrt(ss * (1.0 / HEAD_DIM) + NORM_EPS) * gain

    outs = (q_ref, k_ref, v_ref, xl_ref, gl_ref)
    for idx, o_ref in enumerate(outs):
        z = jnp.dot(n, w_ref[:, idx * D_ATT:(idx + 1) * D_ATT], preferred_element_type=F32)
        if idx < 2:
            z = head_norm(z, qkg_ref[idx:idx + 1, :])
        for p in range(N_GROUPS):
            o_ref[p] = z[:, p * LANES:(p + 1) * LANES].astype(BF16)


def _in_projection(x, mod, ln1_g, w_in_bf, gmat, qkg, tm):
    bsz, t, _ = x.shape
    grouped = jax.ShapeDtypeStruct((bsz, N_GROUPS, t, LANES), BF16)
    out_spec = pl.BlockSpec((None, N_GROUPS, tm, LANES), lambda b, i: (b, 0, i, 0))
    return pl.pallas_call(
        _inproj_kernel,
        grid=(bsz, t // tm),
        in_specs=[
            pl.BlockSpec((None, tm, D_MODEL), lambda b, i: (b, i, 0)),
            pl.BlockSpec((None, SUBLANES, D_MODEL), lambda b, i: (b, 0, 0)),
            _const_spec((1, D_MODEL)),
            _const_spec((D_MODEL, D_IN)),
            _const_spec((D_ATT, D_ATT)),
            _const_spec((SUBLANES, D_ATT)),
        ],
        out_specs=[out_spec] * 5,
        out_shape=[grouped] * 5,
        compiler_params=pltpu.CompilerParams(
            dimension_semantics=("parallel", "parallel"), vmem_limit_bytes=VMEM_LIMIT),
        name="in_projection",
    )(x, mod, ln1_g, w_in_bf, gmat, qkg)


def _attn_kernel(q_ref, k_ref, v_ref, bias_ref, o_ref, *, rows):
    lane = lax.broadcasted_iota(jnp.int32, (GRID_W, LANES), 1)
    low = lane < HEAD_DIM

    def scores(r):
        rs = jnp.clip(r - NA_ROWS // 2, 0, rows - NA_ROWS)
        q0 = pl.multiple_of(r * GRID_W, GRID_W)
        k0 = pl.multiple_of(rs * GRID_W, GRID_W)
        q2 = q_ref[pl.ds(q0, GRID_W), :].astype(F32)
        qq = jnp.concatenate([jnp.where(low, q2, 0.0), jnp.where(low, 0.0, q2)], axis=0).astype(BF16)
        k2 = k_ref[pl.ds(k0, WIN_KEYS), :]
        s = lax.dot_general(qq, k2, (((1,), (1,)), ((), ())), preferred_element_type=F32)
        return s + bias_ref[r - rs], q0, k0

    def softmax(s):
        m = jnp.max(s, axis=-1, keepdims=True)
        e = jnp.exp2(s - m)
        return e.astype(BF16), jnp.sum(e, axis=-1, keepdims=True)

    def weighted(e, l, q0, k0):
        v2 = v_ref[pl.ds(k0, WIN_KEYS), :]
        pv = jnp.dot(e, v2, preferred_element_type=F32)
        pv = pv * (1.0 / l)
        out = jnp.where(low, pv[:GRID_W], pv[GRID_W:])
        o_ref[pl.ds(q0, GRID_W), :] = out.astype(BF16)

    def group_step(g, carry):
        staged = [scores(g * ROW_UNROLL + i) for i in range(ROW_UNROLL)]
        probs = [softmax(s) for s, _, _ in staged]
        for (e, l), (_, q0, k0) in zip(probs, staged):
            weighted(e, l, q0, k0)
        return carry

    lax.fori_loop(0, rows // ROW_UNROLL, group_step, 0)


def _attention(q, k, v, bias):
    bsz, _, t, _ = q.shape
    rows = t // GRID_W
    seq_spec = pl.BlockSpec((None, None, t, LANES), lambda p, b: (b, p, 0, 0))
    return pl.pallas_call(
        functools.partial(_attn_kernel, rows=rows),
        grid=(N_GROUPS, bsz),
        in_specs=[seq_spec, seq_spec, seq_spec,
                  pl.BlockSpec((None, NA_ROWS, 2 * GRID_W, WIN_KEYS), lambda p, b: (p, 0, 0, 0))],
        out_specs=seq_spec,
        out_shape=jax.ShapeDtypeStruct(q.shape, BF16),
        compiler_params=pltpu.CompilerParams(
            dimension_semantics=("parallel", "parallel"), vmem_limit_bytes=VMEM_LIMIT),
        name="nbr_attention",
    )(q, k, v, bias)


def _attention_bias(rpb):
    cols = np.arange(GRID_W)
    col_start = np.clip(cols - NA_COLS // 2, 0, GRID_W - NA_COLS)
    in_win = (cols[None, :] >= col_start[:, None]) & (cols[None, :] < col_start[:, None] + NA_COLS)
    dc = cols[None, :] - cols[:, None] + (NA_COLS - 1)
    onehot = (np.arange(2 * NA_COLS - 1)[:, None, None] == dc[None]) & in_win[None]
    tab = jnp.einsum("hrd,dcx->hrcx", rpb.astype(F32), jnp.asarray(onehot, F32),
                     precision=lax.Precision.HIGHEST)
    tab = jnp.where(jnp.asarray(in_win)[None, None], tab * LOG2_E, MASK_VALUE)
    b = jnp.stack([tab[:, NA_ROWS - 1 - o:2 * NA_ROWS - 1 - o] for o in range(NA_ROWS)], axis=1)
    b = jnp.transpose(b, (0, 1, 3, 2, 4)).reshape(ATT_HEADS, NA_ROWS, GRID_W, WIN_KEYS)
    b = b.reshape(N_GROUPS, 2, NA_ROWS, GRID_W, WIN_KEYS)
    return jnp.transpose(b, (0, 2, 1, 3, 4)).reshape(N_GROUPS, NA_ROWS, 2 * GRID_W, WIN_KEYS)


def _lru_kernel(xl_ref, gl_ref, cw_ref, wf_ref, wb_ref, par_ref, o_ref,
                xp_s, xc_s, hf_s, qf_s, hb_s, qb_s, *, t, tc):
    seg = t // SUBLANES
    pitch = seg + LRU_SEG_PAD
    n_conv = t // tc
    conv_per_seg = seg // tc
    n_steps = seg // LRU_STEPS
    zeros8 = jnp.zeros((SUBLANES, LANES), F32)
    ones8 = jnp.ones((SUBLANES, LANES), F32)
    xp_s[0:SUBLANES, :] = zeros8
    xp_s[SUBLANES + t:, :] = zeros8
    xp_s[SUBLANES:SUBLANES + t, :] = xl_ref[...].astype(F32)

    def softplus(z):
        return jnp.maximum(z, 0.0) + jnp.log(1.0 + jnp.exp(-jnp.abs(z)))

    def gates(xc, w_ref, row0):
        g = jnp.dot(xc.astype(BF16), w_ref[...], preferred_element_type=F32)
        r = _sigmoid(g[:, :LANES] + par_ref[row0:row0 + 1, :])
        i = _sigmoid(g[:, LANES:] + par_ref[row0 + 1:row0 + 2, :])
        log_a = (-LRU_C * softplus(-par_ref[row0 + 2:row0 + 3, :])) * r
        a = jnp.exp(log_a)
        th = jnp.tanh(log_a)
        u = jnp.sqrt(-2.0 * th / (1.0 - th)) * (i * xc)
        return a, u

    def seg_row(c):
        return pl.multiple_of((c // conv_per_seg) * pitch + (c % conv_per_seg) * tc, SUBLANES)

    def conv_step(c, carry):
        t0 = pl.multiple_of(c * tc, tc)
        xe = xp_s[pl.ds(t0, tc + 2 * SUBLANES), :]
        n = tc + 2 * SUBLANES
        acc = cw_ref[1:2, :] * xe[SUBLANES:SUBLANES + tc] + cw_ref[4:5, :]
        for tap, shift in ((0, 1), (2, n - 1), (3, n - 2)):
            acc = acc + cw_ref[tap:tap + 1, :] * pltpu.roll(xe, shift, 0)[SUBLANES:SUBLANES + tc]
        xc_s[pl.ds(seg_row(c), tc), :] = acc
        return carry

    lax.fori_loop(0, n_conv, conv_step, 0)

    def tile(ref, k):
        return ref.at[pl.ds(k, SUBLANES, stride=pitch), :]

    def gather(k0):
        return jnp.concatenate([tile(xc_s, k0 + kk)[...] for kk in range(LRU_STEPS)], axis=0)

    def scan_step(c, carry):
        hf, qf, hb, qb = carry
        kf = c * LRU_STEPS
        kb = (n_steps - 1 - c) * LRU_STEPS
        af, uf = gates(gather(kf), wf_ref, 0)
        ab, ub = gates(gather(kb), wb_ref, 3)
        for kk in range(LRU_STEPS):
            rows = slice(kk * SUBLANES, (kk + 1) * SUBLANES)
            hf = af[rows] * hf + uf[rows]
            qf = af[rows] * qf
            tile(hf_s, kf + kk)[...] = hf
            tile(qf_s, kf + kk)[...] = qf
            kr = LRU_STEPS - 1 - kk
            rows = slice(kr * SUBLANES, (kr + 1) * SUBLANES)
            hb = ab[rows] * hb + ub[rows]
            qb = ab[rows] * qb
            tile(hb_s, kb + kr)[...] = hb
            tile(qb_s, kb + kr)[...] = qb
        return hf, qf, hb, qb

    hf, qf, hb, qb = lax.fori_loop(0, n_steps, scan_step, (zeros8, ones8, zeros8, ones8))

    cf = [jnp.zeros((1, LANES), F32)]
    for s in range(SUBLANES - 1):
        cf.append(qf[s:s + 1] * cf[s] + hf[s:s + 1])
    cb = [jnp.zeros((1, LANES), F32)]
    for s in range(SUBLANES - 1, 0, -1):
        cb.insert(0, qb[s:s + 1] * cb[0] + hb[s:s + 1])

    for c in range(n_conv):
        s = c // conv_per_seg
        r0 = s * pitch + (c % conv_per_seg) * tc
        h = (hf_s[r0:r0 + tc, :] + qf_s[r0:r0 + tc, :] * cf[s]) + (hb_s[r0:r0 + tc, :] + qb_s[r0:r0 + tc, :] * cb[s])
        gate = _gelu_tanh(gl_ref[c * tc:(c + 1) * tc, :].astype(F32))
        o_ref[c * tc:(c + 1) * tc, :] = (h * gate).astype(BF16)


def _rg_lru(xl, gl, cw, wf, wb, par, tc):
    bsz, _, t, _ = xl.shape
    seg = t // SUBLANES
    tc = min(tc, seg)
    assert seg % tc == 0 and seg % LRU_STEPS == 0
    pitched = SUBLANES * (seg + LRU_SEG_PAD)
    seq_spec = pl.BlockSpec((None, None, t, LANES), lambda b, g: (b, g, 0, 0))

    def group_spec(shape):
        return pl.BlockSpec((None,) + shape, lambda b, g: (g,) + (0,) * len(shape))

    return pl.pallas_call(
        functools.partial(_lru_kernel, t=t, tc=tc),
        grid=(bsz, N_GROUPS),
        in_specs=[seq_spec, seq_spec, group_spec((SUBLANES, LANES)), group_spec((LANES, 2 * LANES)),
                  group_spec((LANES, 2 * LANES)), group_spec((SUBLANES, LANES))],
        out_specs=seq_spec,
        out_shape=jax.ShapeDtypeStruct(xl.shape, BF16),
        scratch_shapes=[pltpu.VMEM((t + 2 * SUBLANES, LANES), F32)] + [pltpu.VMEM((pitched, LANES), F32)] * 5,
        compiler_params=pltpu.CompilerParams(
            dimension_semantics=("parallel", "parallel"), vmem_limit_bytes=VMEM_LIMIT),
        name="rg_lru",
    )(xl, gl, cw, wf, wb, par)


def _block_diag_pairs(w):
    w = w.reshape(N_GROUPS, 2, LRU_BLOCK, LRU_BLOCK)
    z = jnp.zeros((N_GROUPS, LRU_BLOCK, LRU_BLOCK), w.dtype)
    top = jnp.concatenate([w[:, 0], z], axis=-1)
    bot = jnp.concatenate([z, w[:, 1]], axis=-1)
    return jnp.concatenate([top, bot], axis=-2)


def _mlp_kernel(x_ref, xp_ref, xn_ref, a_ref, ap_ref, an_ref, l_ref, lp_ref, ln_ref, mod_ref, ln2_ref,
                wout_ref, wup_ref, cw_ref, wdn_ref, o_ref, al_s, x1_s, n2_s, acc_s, u_s, *, tm, t):
    i = pl.program_id(1)
    n_ext = tm + 2 * HALO
    for src, lane0 in (((ap_ref, a_ref, an_ref), 0), ((lp_ref, l_ref, ln_ref), D_ATT)):
        prev_ref, main_ref, next_ref = src
        for p in range(N_GROUPS):
            lanes = slice(lane0 + p * LANES, lane0 + (p + 1) * LANES)
            al_s[0:HALO, lanes] = prev_ref[p]
            al_s[HALO:HALO + tm, lanes] = main_ref[p]
            al_s[HALO + tm:, lanes] = next_ref[p]
    x1_s[0:HALO, :] = xp_ref[...]
    x1_s[HALO:HALO + tm, :] = x_ref[...]
    x1_s[HALO + tm:, :] = xn_ref[...]

    mix = jnp.dot(al_s[...], wout_ref[...], preferred_element_type=F32)
    x1 = x1_s[...] + mod_ref[2:3, :] * mix
    x1_s[...] = x1
    ms = jnp.mean(x1 * x1, axis=-1, keepdims=True)
    n2 = x1 * lax.rsqrt(ms + NORM_EPS) * ln2_ref[...] * (1.0 + mod_ref[4:5, :]) + mod_ref[3:4, :]
    tok = i * tm - HALO + lax.broadcasted_iota(jnp.int32, (n_ext, 1), 0)
    n2_s[...] = jnp.where((tok >= 0) & (tok < t), n2, 0.0).astype(BF16)
    acc_s[...] = jnp.zeros_like(acc_s)

    def up(j, slot):
        n2 = n2_s[...]
        u_s[slot, 0] = jnp.dot(n2, wup_ref[j], preferred_element_type=F32)
        u_s[slot, 1] = jnp.dot(n2, wup_ref[N_FF_CHUNKS + j], preferred_element_type=F32)

    def conv(slot, half, cw):
        c = cw[1:2, :] * u_s[slot, half, HALO:HALO + tm, :] + cw[3:4, :]
        c = c + cw[0:1, :] * u_s[slot, half, HALO - 1:HALO - 1 + tm, :]
        return c + cw[2:3, :] * u_s[slot, half, HALO + 1:HALO + 1 + tm, :]

    def down(j, slot):
        ug = conv(slot, 0, cw_ref[j])
        uv = conv(slot, 1, cw_ref[N_FF_CHUNKS + j])
        act = (_gelu_tanh(ug) * uv).astype(BF16)
        acc_s[...] += jnp.dot(act, wdn_ref[j], preferred_element_type=F32)

    def chunk_pair(k, carry):
        j = 2 * k
        up(j + 1, 1)
        down(j, 0)
        up(j + 2, 0)
        down(j + 1, 1)
        return carry

    assert N_FF_CHUNKS % 2 == 1
    up(0, 0)
    lax.fori_loop(0, N_FF_CHUNKS // 2, chunk_pair, 0)
    down(N_FF_CHUNKS - 1, 0)
    o_ref[...] = x1_s[HALO:HALO + tm, :] + mod_ref[5:6, :] * acc_s[...]


def _out_mlp(x, att, lru, mod, ln2_g, w_out_bf, w_up_c, cw_c, w_dn_c, tm):
    bsz, t, _ = x.shape
    per_halo = tm // HALO
    n_halo = t // HALO

    def main_map(b, i):
        return (b, i, 0)

    def prev_map(b, i):
        return (b, jnp.maximum(i * per_halo - 1, 0), 0)

    def next_map(b, i):
        return (b, jnp.minimum((i + 1) * per_halo, n_halo - 1), 0)

    def grouped(rows, imap):
        return pl.BlockSpec((None, N_GROUPS, rows, LANES), lambda b, i: (imap(b, i)[0], 0, imap(b, i)[1], 0))

    n_ext = tm + 2 * HALO
    return pl.pallas_call(
        functools.partial(_mlp_kernel, tm=tm, t=t),
        grid=(bsz, t // tm),
        in_specs=[
            pl.BlockSpec((None, tm, D_MODEL), main_map),
            pl.BlockSpec((None, HALO, D_MODEL), prev_map),
            pl.BlockSpec((None, HALO, D_MODEL), next_map),
            grouped(tm, main_map), grouped(HALO, prev_map), grouped(HALO, next_map),
            grouped(tm, main_map), grouped(HALO, prev_map), grouped(HALO, next_map),
            pl.BlockSpec((None, SUBLANES, D_MODEL), lambda b, i: (b, 0, 0)),
            _const_spec((1, D_MODEL)),
            _const_spec((D_MODEL, D_MODEL)),
            _const_spec((2 * N_FF_CHUNKS, D_MODEL, FF_CHUNK)),
            _const_spec((2 * N_FF_CHUNKS, SUBLANES, FF_CHUNK)),
            _const_spec((N_FF_CHUNKS, FF_CHUNK, D_MODEL)),
        ],
        out_specs=pl.BlockSpec((None, tm, D_MODEL), main_map),
        out_shape=jax.ShapeDtypeStruct(x.shape, F32),
        scratch_shapes=[pltpu.VMEM((n_ext, D_MODEL), BF16),
                        pltpu.VMEM((n_ext, D_MODEL), F32),
                        pltpu.VMEM((n_ext, D_MODEL), BF16),
                        pltpu.VMEM((tm, D_MODEL), F32),
                        pltpu.VMEM((2, 2, n_ext, FF_CHUNK), F32)],
        compiler_params=pltpu.CompilerParams(
            dimension_semantics=("parallel", "parallel"), vmem_limit_bytes=VMEM_LIMIT),
        name="out_mlp",
    )(x, x, x, att, att, att, lru, lru, lru, mod, ln2_g, w_out_bf, w_up_c, cw_c, w_dn_c)


def _pad_rows(a, rows):
    return jnp.concatenate([a, jnp.zeros((rows - a.shape[0],) + a.shape[1:], a.dtype)], axis=0)


def _prepare_params(ln1_g, ln2_g, w_in, q_norm_g, k_norm_g, rpb, lru_conv_w, lru_conv_b,
                    w_r_f, b_r_f, w_i_f, b_i_f, lam_f, w_r_b, b_r_b, w_i_b, b_i_b, lam_b,
                    w_out, w_up, ffn_conv_w, ffn_conv_b, w_down):
    head = jnp.arange(D_ATT) // HEAD_DIM
    gmat = (head[:, None] == head[None, :]).astype(BF16)
    scale = HEAD_DIM ** -0.5 * LOG2_E
    qkg = _pad_rows(jnp.stack([jnp.tile(q_norm_g.astype(F32), ATT_HEADS) * scale,
                               jnp.tile(k_norm_g.astype(F32), ATT_HEADS)]), SUBLANES)
    lru_cw = _pad_rows(jnp.concatenate([lru_conv_w, lru_conv_b[None]], axis=0), SUBLANES)
    lru_cw = jnp.transpose(lru_cw.reshape(SUBLANES, N_GROUPS, LANES), (1, 0, 2))
    wf = jnp.concatenate([_block_diag_pairs(w_r_f), _block_diag_pairs(w_i_f)], axis=-1).astype(BF16)
    wb = jnp.concatenate([_block_diag_pairs(w_r_b), _block_diag_pairs(w_i_b)], axis=-1).astype(BF16)
    par = _pad_rows(jnp.stack([b_r_f, b_i_f, lam_f, b_r_b, b_i_b, lam_b]), SUBLANES)
    par = jnp.transpose(par.reshape(SUBLANES, N_GROUPS, LANES), (1, 0, 2))
    w_up_c = jnp.transpose(w_up.astype(BF16).reshape(D_MODEL, 2 * N_FF_CHUNKS, FF_CHUNK), (1, 0, 2))
    ffn_cw = _pad_rows(jnp.concatenate([ffn_conv_w, ffn_conv_b[None]], axis=0), SUBLANES)
    ffn_cw = jnp.transpose(ffn_cw.reshape(SUBLANES, 2 * N_FF_CHUNKS, FF_CHUNK), (1, 0, 2))
    return dict(
        ln1_g=ln1_g.reshape(1, D_MODEL), ln2_g=ln2_g.reshape(1, D_MODEL),
        w_in=w_in.astype(BF16), gmat=gmat, qkg=qkg, bias=_attention_bias(rpb),
        lru_cw=lru_cw, wf=wf, wb=wb, par=par,
        w_out=w_out.astype(BF16), w_up_c=w_up_c, ffn_cw=ffn_cw,
        w_dn_c=w_down.astype(BF16).reshape(N_FF_CHUNKS, FF_CHUNK, D_MODEL))


def _encoder_layer(x, mod, p, tm, tc):
    q, k, v, xl, gl = _in_projection(x, mod, p["ln1_g"], p["w_in"], p["gmat"], p["qkg"], tm)
    att = _attention(q, k, v, p["bias"])
    lru = _rg_lru(xl, gl, p["lru_cw"], p["wf"], p["wb"], p["par"], tc)
    return _out_mlp(x, att, lru, mod, p["ln2_g"], p["w_out"], p["w_up_c"], p["ffn_cw"], p["w_dn_c"], tm)


def _forward(x_prompt, x_sample, c_prompt, c_sample, w_ada, b_ada, layer_params, tm=512, tc=512):
    nb_p, nb_s = c_prompt.shape[0], c_sample.shape[0]
    depth = w_ada.shape[0]
    for layer in range(depth):
        c_all = jnp.concatenate([c_prompt, c_sample], axis=0)
        c_all = _pad_rows(c_all, -(-c_all.shape[0] // SUBLANES) * SUBLANES)
        mod = _modulation(c_all, w_ada[layer], b_ada[layer])
        mod = mod.reshape(mod.shape[0], N_MOD, D_MODEL)
        mod = jnp.concatenate([mod, jnp.zeros((mod.shape[0], SUBLANES - N_MOD, D_MODEL), F32)], axis=1)
        p = _prepare_params(*[w[layer] for w in layer_params])
        x_prompt = _encoder_layer(x_prompt, mod[:nb_p], p, tm, tc)
        x_sample = _encoder_layer(x_sample, mod[nb_p:nb_p + nb_s], p, tm, tc)
    return x_prompt, x_sample


def kernel(x_prompt, x_sample, c_prompt, c_sample, ln1_g, ln2_g, w_ada, b_ada, w_in, q_norm_g, k_norm_g, rpb,
           lru_conv_w, lru_conv_b, w_r_f, b_r_f, w_i_f, b_i_f, lam_f, w_r_b, b_r_b, w_i_b, b_i_b, lam_b,
           w_out, w_up, ffn_conv_w, ffn_conv_b, w_down):
    layer_params = (ln1_g, ln2_g, w_in, q_norm_g, k_norm_g, rpb, lru_conv_w, lru_conv_b,
                    w_r_f, b_r_f, w_i_f, b_i_f, lam_f, w_r_b, b_r_b, w_i_b, b_i_b, lam_b,
                    w_out, w_up, ffn_conv_w, ffn_conv_b, w_down)
    return _forward(x_prompt, x_sample, c_prompt, c_sample, w_ada, b_ada, layer_params)
```

```python
import functools
import math

import numpy as np

import jax
import jax.numpy as jnp
from jax import lax
from jax.experimental import pallas as pl
from jax.experimental.pallas import tpu as pltpu

D_MODEL = 1024
GRID_W = 64
ATT_HEADS = 8
HEAD_DIM = 64
D_ATT = ATT_HEADS * HEAD_DIM
D_LRU = 512
LRU_HEADS = 8
LRU_BLOCK = D_LRU // LRU_HEADS
NA_ROWS = 8
NA_COLS = 16
LRU_CONV = 4
LRU_C = 8.0
FFN_CONV = 3
D_FF = 2816
D_IN = 3 * D_ATT + 2 * D_LRU
N_MOD = 6
NORM_EPS = 1e-6

LANES = 128
SUBLANES = 8
MXU_WIDTH = 256
N_GROUPS = D_ATT // LANES
WIN_KEYS = NA_ROWS * GRID_W
FF_CHUNK = 256
N_FF_CHUNKS = D_FF // FF_CHUNK
HALO = 16
PROLOGUE_BLOCKS = 3
DOWN_BLOCKS = 2
ROW_UNROLL = 8
LRU_STEPS = 64
LRU_SEG_PAD = 8
MASK_VALUE = -1e30
LOG2_E = math.log2(math.e)
TINY = 1e-30
VMEM_LIMIT = 56 * 1024 * 1024

BF16 = jnp.bfloat16
F32 = jnp.float32


def _gelu_tanh(x):
    return 0.5 * x * (1.0 + jnp.tanh(math.sqrt(2.0 / math.pi) * (x + 0.044715 * (x * x * x))))


def _sigmoid(x):
    return 0.5 * jnp.tanh(0.5 * x) + 0.5


def _const_spec(shape):
    zeros = (0,) * len(shape)
    return pl.BlockSpec(shape, lambda *_: zeros, pipeline_mode=pl.Buffered(1))


def _mod_kernel(c_ref, w_ref, b_ref, o_ref):
    c = c_ref[...]
    s = (c * _sigmoid(c)).astype(BF16)
    o_ref[...] = jnp.dot(s, w_ref[...].astype(BF16), preferred_element_type=F32) + b_ref[...]


def _modulation(c_all, w_ada, b_ada):
    bp = c_all.shape[0]
    return pl.pallas_call(
        _mod_kernel,
        grid=(N_MOD,),
        in_specs=[
            pl.BlockSpec((bp, D_MODEL), lambda j: (0, 0)),
            pl.BlockSpec((D_MODEL, D_MODEL), lambda j: (0, j)),
            pl.BlockSpec((1, D_MODEL), lambda j: (0, j)),
        ],
        out_specs=pl.BlockSpec((bp, D_MODEL), lambda j: (0, j)),
        out_shape=jax.ShapeDtypeStruct((bp, N_MOD * D_MODEL), F32),
        compiler_params=pltpu.CompilerParams(dimension_semantics=("arbitrary",), vmem_limit_bytes=VMEM_LIMIT),
        name="modulation",
    )(c_all, w_ada, b_ada.reshape(1, -1))


def _inproj_kernel(x_ref, mod_ref, ln_ref, w_ref, gmat_ref, qkg_ref,
                   q_ref, k_ref, v_ref, xl_ref, gl_ref):
    x = x_ref[...]
    ms = jnp.mean(x * x, axis=-1, keepdims=True)
    y = x * lax.rsqrt(ms + NORM_EPS) * ln_ref[...]
    n = (y * (1.0 + mod_ref[1:2, :]) + mod_ref[0:1, :]).astype(BF16)

    def head_norm(z, gain):
        zz = (z * z).astype(BF16)
        width = gmat_ref.shape[0]
        ss = jnp.concatenate(
            [jnp.dot(zz[:, c:c + width], gmat_ref[...], preferred_element_type=F32) for c in range(0, D_ATT, width)],
            axis=1)
        return z * lax.rsqrt(ss * (1.0 / HEAD_DIM) + NORM_EPS) * gain

    outs = (q_ref, k_ref, v_ref, xl_ref, gl_ref)
    for idx, o_ref in enumerate(outs):
        z = jnp.dot(n, w_ref[:, idx * D_ATT:(idx + 1) * D_ATT], preferred_element_type=F32)
        if idx < 2:
            z = head_norm(z, qkg_ref[idx:idx + 1, :])
        for p in range(N_GROUPS):
            o_ref[p] = z[:, p * LANES:(p + 1) * LANES].astype(BF16)


def _in_projection(x, mod, ln1_g, w_in_bf, gmat, qkg, tm):
    bsz, t, _ = x.shape
    grouped = jax.ShapeDtypeStruct((bsz, N_GROUPS, t, LANES), BF16)
    out_spec = pl.BlockSpec((None, N_GROUPS, tm, LANES), lambda b, i: (b, 0, i, 0))
    return pl.pallas_call(
        _inproj_kernel,
        grid=(bsz, t // tm),
        in_specs=[
            pl.BlockSpec((None, tm, D_MODEL), lambda b, i: (b, i, 0)),
            pl.BlockSpec((None, SUBLANES, D_MODEL), lambda b, i: (b, 0, 0)),
            _const_spec((1, D_MODEL)),
            _const_spec((D_MODEL, D_IN)),
            _const_spec((MXU_WIDTH, MXU_WIDTH)),
            _const_spec((SUBLANES, D_ATT)),
        ],
        out_specs=[out_spec] * 5,
        out_shape=[grouped] * 5,
        compiler_params=pltpu.CompilerParams(
            dimension_semantics=("parallel", "parallel"), vmem_limit_bytes=VMEM_LIMIT),
        name="in_projection",
    )(x, mod, ln1_g, w_in_bf, gmat, qkg)


def _attn_kernel(q_ref, k_ref, v_ref, bias_ref, o_ref, *, rows):
    lane = lax.broadcasted_iota(jnp.int32, (GRID_W, LANES), 1)
    low = lane < HEAD_DIM

    def scores(r):
        rs = jnp.clip(r - NA_ROWS // 2, 0, rows - NA_ROWS)
        q0 = pl.multiple_of(r * GRID_W, GRID_W)
        k0 = pl.multiple_of(rs * GRID_W, GRID_W)
        q2 = q_ref[pl.ds(q0, GRID_W), :].astype(F32)
        qq = jnp.concatenate([jnp.where(low, q2, 0.0), jnp.where(low, 0.0, q2)], axis=0).astype(BF16)
        k2 = k_ref[pl.ds(k0, WIN_KEYS), :]
        s = lax.dot_general(qq, k2, (((1,), (1,)), ((), ())), preferred_element_type=F32)
        return s + bias_ref[r - rs], q0, k0

    def softmax(s):
        m = jnp.max(s, axis=-1, keepdims=True)
        e = jnp.exp2(s - m)
        return e.astype(BF16), jnp.sum(e, axis=-1, keepdims=True)

    def weighted(e, l, q0, k0):
        v2 = v_ref[pl.ds(k0, WIN_KEYS), :]
        pv = jnp.dot(e, v2, preferred_element_type=F32)
        pv = pv * (1.0 / l)
        out = jnp.where(low, pv[:GRID_W], pv[GRID_W:])
        o_ref[pl.ds(q0, GRID_W), :] = out.astype(BF16)

    def group_step(g, carry):
        staged = [scores(g * ROW_UNROLL + i) for i in range(ROW_UNROLL)]
        probs = [softmax(s) for s, _, _ in staged]
        for (e, l), (_, q0, k0) in zip(probs, staged):
            weighted(e, l, q0, k0)
        return carry

    lax.fori_loop(0, rows // ROW_UNROLL, group_step, 0)


def _attention(q, k, v, bias):
    bsz, _, t, _ = q.shape
    rows = t // GRID_W
    seq_spec = pl.BlockSpec((None, None, t, LANES), lambda p, b: (b, p, 0, 0))
    return pl.pallas_call(
        functools.partial(_attn_kernel, rows=rows),
        grid=(N_GROUPS, bsz),
        in_specs=[seq_spec, seq_spec, seq_spec,
                  pl.BlockSpec((None, NA_ROWS, 2 * GRID_W, WIN_KEYS), lambda p, b: (p, 0, 0, 0))],
        out_specs=seq_spec,
        out_shape=jax.ShapeDtypeStruct(q.shape, BF16),
        compiler_params=pltpu.CompilerParams(
            dimension_semantics=("parallel", "parallel"), vmem_limit_bytes=VMEM_LIMIT),
        name="nbr_attention",
    )(q, k, v, bias)


def _attention_bias(rpb):
    cols = np.arange(GRID_W)
    col_start = np.clip(cols - NA_COLS // 2, 0, GRID_W - NA_COLS)
    in_win = (cols[None, :] >= col_start[:, None]) & (cols[None, :] < col_start[:, None] + NA_COLS)
    dc = cols[None, :] - cols[:, None] + (NA_COLS - 1)
    onehot = (np.arange(2 * NA_COLS - 1)[:, None, None] == dc[None]) & in_win[None]
    tab = jnp.einsum("hrd,dcx->hrcx", rpb.astype(F32), jnp.asarray(onehot, F32),
                     precision=lax.Precision.HIGHEST)
    tab = jnp.where(jnp.asarray(in_win)[None, None], tab * LOG2_E, MASK_VALUE)
    b = jnp.stack([tab[:, NA_ROWS - 1 - o:2 * NA_ROWS - 1 - o] for o in range(NA_ROWS)], axis=1)
    b = jnp.transpose(b, (0, 1, 3, 2, 4)).reshape(ATT_HEADS, NA_ROWS, GRID_W, WIN_KEYS)
    b = b.reshape(N_GROUPS, 2, NA_ROWS, GRID_W, WIN_KEYS)
    return jnp.transpose(b, (0, 2, 1, 3, 4)).reshape(N_GROUPS, NA_ROWS, 2 * GRID_W, WIN_KEYS)


def _lru_kernel(xl_ref, gl_ref, cw_ref, wf_ref, wb_ref, par_ref, o_ref,
                xp_s, xc_s, hf_s, qf_s, hb_s, qb_s, *, t, tc):
    seg = t // SUBLANES
    pitch = seg + LRU_SEG_PAD
    n_conv = t // tc
    conv_per_seg = seg // tc
    n_steps = seg // LRU_STEPS
    zeros8 = jnp.zeros((SUBLANES, LANES), F32)
    ones8 = jnp.ones((SUBLANES, LANES), F32)
    xp_s[0:SUBLANES, :] = zeros8
    xp_s[SUBLANES + t:, :] = zeros8
    xp_s[SUBLANES:SUBLANES + t, :] = xl_ref[...].astype(F32)

    def softplus(z):
        return jnp.maximum(z, 0.0) + jnp.log(1.0 + jnp.exp(-jnp.abs(z)))

    half_decay = [(-0.5 * LRU_C) * softplus(-par_ref[row:row + 1, :]) for row in (2, 5)]

    def gates(xc, w_ref, direction):
        row0 = 3 * direction
        g = jnp.dot(xc.astype(BF16), w_ref[...], preferred_element_type=F32)
        tr = jnp.tanh(g[:, :LANES] + par_ref[row0:row0 + 1, :])
        ti = jnp.tanh(g[:, LANES:] + par_ref[row0 + 1:row0 + 2, :])
        log_a = half_decay[direction] * tr + half_decay[direction]
        a = jnp.exp(log_a)
        beta_sq = (a * a + 1.0) * (-jnp.tanh(log_a))
        beta = beta_sq * lax.rsqrt(jnp.maximum(beta_sq, TINY))
        u = beta * ((0.5 * ti + 0.5) * xc)
        return a, u

    def seg_row(c):
        return pl.multiple_of((c // conv_per_seg) * pitch + (c % conv_per_seg) * tc, SUBLANES)

    def conv_step(c, carry):
        t0 = pl.multiple_of(c * tc, tc)
        xe = xp_s[pl.ds(t0, tc + 2 * SUBLANES), :]
        n = tc + 2 * SUBLANES
        acc = cw_ref[1:2, :] * xe[SUBLANES:SUBLANES + tc] + cw_ref[4:5, :]
        for tap, shift in ((0, 1), (2, n - 1), (3, n - 2)):
            acc = acc + cw_ref[tap:tap + 1, :] * pltpu.roll(xe, shift, 0)[SUBLANES:SUBLANES + tc]
        xc_s[pl.ds(seg_row(c), tc), :] = acc
        return carry

    lax.fori_loop(0, n_conv, conv_step, 0)

    def tile(ref, k):
        return ref.at[pl.ds(k, SUBLANES, stride=pitch), :]

    def gather(k0):
        return jnp.concatenate([tile(xc_s, k0 + kk)[...] for kk in range(LRU_STEPS)], axis=0)

    def scan_step(c, carry):
        hf, qf, hb, qb = carry
        kf = c * LRU_STEPS
        kb = (n_steps - 1 - c) * LRU_STEPS
        af, uf = gates(gather(kf), wf_ref, 0)
        ab, ub = gates(gather(kb), wb_ref, 1)
        for kk in range(LRU_STEPS):
            rows = slice(kk * SUBLANES, (kk + 1) * SUBLANES)
            hf = af[rows] * hf + uf[rows]
            qf = af[rows] * qf
            tile(hf_s, kf + kk)[...] = hf
            tile(qf_s, kf + kk)[...] = qf
            kr = LRU_STEPS - 1 - kk
            rows = slice(kr * SUBLANES, (kr + 1) * SUBLANES)
            hb = ab[rows] * hb + ub[rows]
            qb = ab[rows] * qb
            tile(hb_s, kb + kr)[...] = hb
            tile(qb_s, kb + kr)[...] = qb
        return hf, qf, hb, qb

    hf, qf, hb, qb = lax.fori_loop(0, n_steps, scan_step, (zeros8, ones8, zeros8, ones8))

    cf = [jnp.zeros((1, LANES), F32)]
    for s in range(SUBLANES - 1):
        cf.append(qf[s:s + 1] * cf[s] + hf[s:s + 1])
    cb = [jnp.zeros((1, LANES), F32)]
    for s in range(SUBLANES - 1, 0, -1):
        cb.insert(0, qb[s:s + 1] * cb[0] + hb[s:s + 1])

    for c in range(n_conv):
        s = c // conv_per_seg
        r0 = s * pitch + (c % conv_per_seg) * tc
        h = (hf_s[r0:r0 + tc, :] + qf_s[r0:r0 + tc, :] * cf[s]) + (hb_s[r0:r0 + tc, :] + qb_s[r0:r0 + tc, :] * cb[s])
        gate = _gelu_tanh(gl_ref[c * tc:(c + 1) * tc, :].astype(F32))
        o_ref[c * tc:(c + 1) * tc, :] = (h * gate).astype(BF16)


def _rg_lru(xl, gl, cw, wf, wb, par, tc):
    bsz, _, t, _ = xl.shape
    seg = t // SUBLANES
    tc = min(tc, seg)
    assert seg % tc == 0 and seg % LRU_STEPS == 0
    pitched = SUBLANES * (seg + LRU_SEG_PAD)
    seq_spec = pl.BlockSpec((None, None, t, LANES), lambda b, g: (b, g, 0, 0))

    def group_spec(shape):
        return pl.BlockSpec((None,) + shape, lambda b, g: (g,) + (0,) * len(shape))

    return pl.pallas_call(
        functools.partial(_lru_kernel, t=t, tc=tc),
        grid=(bsz, N_GROUPS),
        in_specs=[seq_spec, seq_spec, group_spec((SUBLANES, LANES)), group_spec((LANES, 2 * LANES)),
                  group_spec((LANES, 2 * LANES)), group_spec((SUBLANES, LANES))],
        out_specs=seq_spec,
        out_shape=jax.ShapeDtypeStruct(xl.shape, BF16),
        scratch_shapes=[pltpu.VMEM((t + 2 * SUBLANES, LANES), F32)] + [pltpu.VMEM((pitched, LANES), F32)] * 5,
        compiler_params=pltpu.CompilerParams(
            dimension_semantics=("parallel", "parallel"), vmem_limit_bytes=VMEM_LIMIT),
        name="rg_lru",
    )(xl, gl, cw, wf, wb, par)


def _block_diag_pairs(w):
    w = w.reshape(N_GROUPS, 2, LRU_BLOCK, LRU_BLOCK)
    z = jnp.zeros((N_GROUPS, LRU_BLOCK, LRU_BLOCK), w.dtype)
    top = jnp.concatenate([w[:, 0], z], axis=-1)
    bot = jnp.concatenate([z, w[:, 1]], axis=-1)
    return jnp.concatenate([top, bot], axis=-2)


def _mlp_kernel(x_ref, xp_ref, xn_ref, a_ref, ap_ref, an_ref, l_ref, lp_ref, ln_ref, mod_ref, ln2_ref,
                wout_ref, wup_ref, cw_ref, wdn_ref, o_ref, al_s, x1_s, n2_s, acc_s,
                ug0_s, uv0_s, ug1_s, uv1_s, *, tm):
    u_s = ((ug0_s, uv0_s), (ug1_s, uv1_s))
    i = pl.program_id(1)
    n_ext = tm + HALO
    first_half = lax.broadcasted_iota(jnp.int32, (HALO, 1), 0) < HALO // 2

    def edge_tile(next_rows, prev_rows):
        return jnp.where(first_half, next_rows.astype(F32), prev_rows.astype(F32))

    for (prev_ref, main_ref, next_ref), lane0 in (((ap_ref, a_ref, an_ref), 0), ((lp_ref, l_ref, ln_ref), D_ATT)):
        for p in range(N_GROUPS):
            lanes = slice(lane0 + p * LANES, lane0 + (p + 1) * LANES)
            al_s[0:HALO, lanes] = edge_tile(next_ref[p], prev_ref[p]).astype(BF16)
            al_s[HALO:, lanes] = main_ref[p]
    x_edge = edge_tile(xn_ref[...], xp_ref[...])
    has_next = (i < pl.num_programs(1) - 1).astype(jnp.int32)
    has_prev = (i > 0).astype(jnp.int32)

    rows_per_block = n_ext // PROLOGUE_BLOCKS
    assert rows_per_block * PROLOGUE_BLOCKS == n_ext and rows_per_block % HALO == 0
    for blk in range(PROLOGUE_BLOCKS):
        r0, r1 = blk * rows_per_block, (blk + 1) * rows_per_block
        mix = jnp.dot(al_s[r0:r1, :], wout_ref[...], preferred_element_type=F32)
        if blk == 0:
            xb = jnp.concatenate([x_edge, x_ref[0:r1 - HALO, :]], axis=0)
        else:
            xb = x_ref[r0 - HALO:r1 - HALO, :]
        x1 = xb + mod_ref[2:3, :] * mix
        x1_s[max(r0 - HALO, 0):r1 - HALO, :] = x1[HALO:] if blk == 0 else x1
        ms = jnp.mean(x1 * x1, axis=-1, keepdims=True)
        n2 = x1 * lax.rsqrt(ms + NORM_EPS) * ln2_ref[...] * (1.0 + mod_ref[4:5, :]) + mod_ref[3:4, :]
        if blk == 0:
            row = lax.broadcasted_iota(jnp.int32, (r1 - r0, 1), 0)
            edge_ok = jnp.where(row < HALO // 2, has_next, has_prev)
            n2 = jnp.where((row >= HALO) | (edge_ok > 0), n2, 0.0)
        n2_s[r0:r1, :] = n2.astype(BF16)

    def up(j, slot):
        n2 = n2_s[...]
        for half, col in ((0, j), (1, N_FF_CHUNKS + j)):
            u = jnp.dot(n2, wup_ref[col], preferred_element_type=F32)
            u_s[slot][half][0:n_ext, :] = u
            u_s[slot][half][n_ext:, :] = u[0:HALO]

    def conv(u_ref, cw, r0, rows):
        c = cw[1:2, :] * u_ref[HALO + r0:HALO + r0 + rows, :] + cw[3:4, :]
        c = c + cw[0:1, :] * u_ref[HALO + r0 - 1:HALO + r0 - 1 + rows, :]
        return c + cw[2:3, :] * u_ref[HALO + r0 + 1:HALO + r0 + 1 + rows, :]

    def down(j, slot):
        rows = tm // DOWN_BLOCKS
        for blk in range(DOWN_BLOCKS):
            r0 = blk * rows
            ug = conv(u_s[slot][0], cw_ref[j], r0, rows)
            uv = conv(u_s[slot][1], cw_ref[N_FF_CHUNKS + j], r0, rows)
            act = (_gelu_tanh(ug) * uv).astype(BF16)
            acc_s[r0:r0 + rows, :] += jnp.dot(act, wdn_ref[j], preferred_element_type=F32)

    def chunk_pair(k, carry):
        j = 2 * k
        up(j + 1, 1)
        down(j, 0)
        up(j + 2, 0)
        down(j + 1, 1)
        return carry

    assert N_FF_CHUNKS % 2 == 1
    acc_s[...] = jnp.zeros_like(acc_s)
    up(0, 0)
    lax.fori_loop(0, N_FF_CHUNKS // 2, chunk_pair, 0)
    down(N_FF_CHUNKS - 1, 0)
    o_ref[...] = x1_s[...] + mod_ref[5:6, :] * acc_s[...]


def _out_mlp(x, att, lru, mod, ln2_g, w_out_bf, w_up_c, cw_c, w_dn_c, tm):
    bsz, t, _ = x.shape
    per_halo = tm // HALO
    n_halo = t // HALO

    def main_map(b, i):
        return (b, i, 0)

    def prev_map(b, i):
        return (b, jnp.maximum(i * per_halo - 1, 0), 0)

    def next_map(b, i):
        return (b, jnp.minimum((i + 1) * per_halo, n_halo - 1), 0)

    def grouped(rows, imap):
        return pl.BlockSpec((None, N_GROUPS, rows, LANES), lambda b, i: (imap(b, i)[0], 0, imap(b, i)[1], 0))

    n_ext = tm + HALO
    return pl.pallas_call(
        functools.partial(_mlp_kernel, tm=tm),
        grid=(bsz, t // tm),
        in_specs=[
            pl.BlockSpec((None, tm, D_MODEL), main_map),
            pl.BlockSpec((None, HALO, D_MODEL), prev_map),
            pl.BlockSpec((None, HALO, D_MODEL), next_map),
            grouped(tm, main_map), grouped(HALO, prev_map), grouped(HALO, next_map),
            grouped(tm, main_map), grouped(HALO, prev_map), grouped(HALO, next_map),
            pl.BlockSpec((None, SUBLANES, D_MODEL), lambda b, i: (b, 0, 0)),
            _const_spec((1, D_MODEL)),
            _const_spec((D_MODEL, D_MODEL)),
            _const_spec((2 * N_FF_CHUNKS, D_MODEL, FF_CHUNK)),
            _const_spec((2 * N_FF_CHUNKS, SUBLANES, FF_CHUNK)),
            _const_spec((N_FF_CHUNKS, FF_CHUNK, D_MODEL)),
        ],
        out_specs=pl.BlockSpec((None, tm, D_MODEL), main_map),
        out_shape=jax.ShapeDtypeStruct(x.shape, F32),
        scratch_shapes=[pltpu.VMEM((n_ext, D_MODEL), BF16),
                        pltpu.VMEM((tm, D_MODEL), F32),
                        pltpu.VMEM((n_ext, D_MODEL), BF16),
                        pltpu.VMEM((tm, D_MODEL), F32),
                        ] + [pltpu.VMEM((n_ext + HALO, FF_CHUNK), F32)] * 4,
        compiler_params=pltpu.CompilerParams(
            dimension_semantics=("parallel", "parallel"), vmem_limit_bytes=VMEM_LIMIT),
        name="out_mlp",
    )(x, x, x, att, att, att, lru, lru, lru, mod, ln2_g, w_out_bf, w_up_c, cw_c, w_dn_c)


def _pad_rows(a, rows):
    return jnp.concatenate([a, jnp.zeros((rows - a.shape[0],) + a.shape[1:], a.dtype)], axis=0)


def _prepare_params(ln1_g, ln2_g, w_in, q_norm_g, k_norm_g, rpb, lru_conv_w, lru_conv_b,
                    w_r_f, b_r_f, w_i_f, b_i_f, lam_f, w_r_b, b_r_b, w_i_b, b_i_b, lam_b,
                    w_out, w_up, ffn_conv_w, ffn_conv_b, w_down):
    head = jnp.arange(MXU_WIDTH) // HEAD_DIM
    gmat = (head[:, None] == head[None, :]).astype(BF16)
    scale = HEAD_DIM ** -0.5 * LOG2_E
    qkg = _pad_rows(jnp.stack([jnp.tile(q_norm_g.astype(F32), ATT_HEADS) * scale,
                               jnp.tile(k_norm_g.astype(F32), ATT_HEADS)]), SUBLANES)
    lru_cw = _pad_rows(jnp.concatenate([lru_conv_w, lru_conv_b[None]], axis=0), SUBLANES)
    lru_cw = jnp.transpose(lru_cw.reshape(SUBLANES, N_GROUPS, LANES), (1, 0, 2))
    wf = (0.5 * jnp.concatenate([_block_diag_pairs(w_r_f), _block_diag_pairs(w_i_f)], axis=-1)).astype(BF16)
    wb = (0.5 * jnp.concatenate([_block_diag_pairs(w_r_b), _block_diag_pairs(w_i_b)], axis=-1)).astype(BF16)
    par = _pad_rows(jnp.stack([0.5 * b_r_f, 0.5 * b_i_f, lam_f, 0.5 * b_r_b, 0.5 * b_i_b, lam_b]), SUBLANES)
    par = jnp.transpose(par.reshape(SUBLANES, N_GROUPS, LANES), (1, 0, 2))
    w_up_c = jnp.transpose(w_up.astype(BF16).reshape(D_MODEL, 2 * N_FF_CHUNKS, FF_CHUNK), (1, 0, 2))
    ffn_cw = _pad_rows(jnp.concatenate([ffn_conv_w, ffn_conv_b[None]], axis=0), SUBLANES)
    ffn_cw = jnp.transpose(ffn_cw.reshape(SUBLANES, 2 * N_FF_CHUNKS, FF_CHUNK), (1, 0, 2))
    return dict(
        ln1_g=ln1_g.reshape(1, D_MODEL), ln2_g=ln2_g.reshape(1, D_MODEL),
        w_in=w_in.astype(BF16), gmat=gmat, qkg=qkg, bias=_attention_bias(rpb),
        lru_cw=lru_cw, wf=wf, wb=wb, par=par,
        w_out=w_out.astype(BF16), w_up_c=w_up_c, ffn_cw=ffn_cw,
        w_dn_c=w_down.astype(BF16).reshape(N_FF_CHUNKS, FF_CHUNK, D_MODEL))


def _encoder_layer(x, mod, p, tm, tc):
    q, k, v, xl, gl = _in_projection(x, mod, p["ln1_g"], p["w_in"], p["gmat"], p["qkg"], tm)
    att = _attention(q, k, v, p["bias"])
    lru = _rg_lru(xl, gl, p["lru_cw"], p["wf"], p["wb"], p["par"], tc)
    return _out_mlp(x, att, lru, mod, p["ln2_g"], p["w_out"], p["w_up_c"], p["ffn_cw"], p["w_dn_c"], tm)


def _forward(x_prompt, x_sample, c_prompt, c_sample, w_ada, b_ada, layer_params, tm=512, tc=512):
    nb_p, nb_s = c_prompt.shape[0], c_sample.shape[0]
    depth = w_ada.shape[0]
    for layer in range(depth):
        c_all = jnp.concatenate([c_prompt, c_sample], axis=0)
        c_all = _pad_rows(c_all, -(-c_all.shape[0] // SUBLANES) * SUBLANES)
        mod = _modulation(c_all, w_ada[layer], b_ada[layer])
        mod = mod.reshape(mod.shape[0], N_MOD, D_MODEL)
        mod = jnp.concatenate([mod, jnp.zeros((mod.shape[0], SUBLANES - N_MOD, D_MODEL), F32)], axis=1)
        p = _prepare_params(*[w[layer] for w in layer_params])
        x_prompt = _encoder_layer(x_prompt, mod[:nb_p], p, tm, tc)
        x_sample = _encoder_layer(x_sample, mod[nb_p:nb_p + nb_s], p, tm, tc)
    return x_prompt, x_sample


def kernel(x_prompt, x_sample, c_prompt, c_sample, ln1_g, ln2_g, w_ada, b_ada, w_in, q_norm_g, k_norm_g, rpb,
           lru_conv_w, lru_conv_b, w_r_f, b_r_f, w_i_f, b_i_f, lam_f, w_r_b, b_r_b, w_i_b, b_i_b, lam_b,
           w_out, w_up, ffn_conv_w, ffn_conv_b, w_down):
    layer_params = (ln1_g, ln2_g, w_in, q_norm_g, k_norm_g, rpb, lru_conv_w, lru_conv_b,
                    w_r_f, b_r_f, w_i_f, b_i_f, lam_f, w_r_b, b_r_b, w_i_b, b_i_b, lam_b,
                    w_out, w_up, ffn_conv_w, ffn_conv_b, w_down)
    return _forward(x_prompt, x_sample, c_prompt, c_sample, w_ada, b_ada, layer_params)
```

```python
import functools
import math

import numpy as np

import jax
import jax.numpy as jnp
from jax import lax
from jax.experimental import pallas as pl
from jax.experimental.pallas import tpu as pltpu

D_MODEL = 1024
GRID_W = 64
ATT_HEADS = 8
HEAD_DIM = 64
D_ATT = ATT_HEADS * HEAD_DIM
D_LRU = 512
LRU_HEADS = 8
LRU_BLOCK = D_LRU // LRU_HEADS
NA_ROWS = 8
NA_COLS = 16
LRU_CONV = 4
LRU_C = 8.0
FFN_CONV = 3
D_FF = 2816
D_IN = 3 * D_ATT + 2 * D_LRU
N_MOD = 6
NORM_EPS = 1e-6

LANES = 128
SUBLANES = 8
MXU_WIDTH = 256
N_GROUPS = D_ATT // LANES
WIN_KEYS = NA_ROWS * GRID_W
FF_CHUNKS = ((0, 6 * MXU_WIDTH), (6 * MXU_WIDTH, D_FF))
FF_CHUNK_MAX = max(c1 - c0 for c0, c1 in FF_CHUNKS)
HALO = 16
PROLOGUE_BLOCKS = 2
ROW_UNROLL = 8
LRU_STEPS = 64
LRU_SEG_PAD = 8
MASK_VALUE = -1e30
LOG2_E = math.log2(math.e)
TINY = 1e-30
VMEM_LIMIT = 56 * 1024 * 1024

BF16 = jnp.bfloat16
F32 = jnp.float32


def _gelu_tanh(x):
    return 0.5 * x * (1.0 + jnp.tanh(math.sqrt(2.0 / math.pi) * (x + 0.044715 * (x * x * x))))


def _gelu_inner_tanh(x):
    c = math.sqrt(2.0 / math.pi)
    return jnp.tanh(x * (c + (0.044715 * c) * (x * x)))


def _sigmoid(x):
    return 0.5 * jnp.tanh(0.5 * x) + 0.5


def _const_spec(shape):
    zeros = (0,) * len(shape)
    return pl.BlockSpec(shape, lambda *_: zeros, pipeline_mode=pl.Buffered(1))


def _mod_kernel(c_ref, w_ref, b_ref, o_ref):
    c = c_ref[...]
    s = (c * _sigmoid(c)).astype(BF16)
    o_ref[...] = jnp.dot(s, w_ref[...].astype(BF16), preferred_element_type=F32) + b_ref[...]


def _modulation(c_all, w_ada, b_ada):
    bp = c_all.shape[0]
    return pl.pallas_call(
        _mod_kernel,
        grid=(N_MOD,),
        in_specs=[
            pl.BlockSpec((bp, D_MODEL), lambda j: (0, 0)),
            pl.BlockSpec((D_MODEL, D_MODEL), lambda j: (0, j)),
            pl.BlockSpec((1, D_MODEL), lambda j: (0, j)),
        ],
        out_specs=pl.BlockSpec((bp, D_MODEL), lambda j: (0, j)),
        out_shape=jax.ShapeDtypeStruct((bp, N_MOD * D_MODEL), F32),
        compiler_params=pltpu.CompilerParams(dimension_semantics=("arbitrary",), vmem_limit_bytes=VMEM_LIMIT),
        name="modulation",
    )(c_all, w_ada, b_ada.reshape(1, -1))


def _inproj_kernel(x_ref, mod_ref, ln_ref, w_ref, gmat_ref, qkg_ref,
                   q_ref, k_ref, v_ref, xl_ref, gl_ref):
    x = x_ref[...]
    ms = jnp.mean(x * x, axis=-1, keepdims=True)
    y = x * lax.rsqrt(ms + NORM_EPS) * ln_ref[...]
    n = (y * (1.0 + mod_ref[1:2, :]) + mod_ref[0:1, :]).astype(BF16)

    def head_norm(z, gain):
        zz = (z * z).astype(BF16)
        width = gmat_ref.shape[0]
        ss = jnp.concatenate(
            [jnp.dot(zz[:, c:c + width], gmat_ref[...], preferred_element_type=F32) for c in range(0, D_ATT, width)],
            axis=1)
        return z * lax.rsqrt(ss * (1.0 / HEAD_DIM) + NORM_EPS) * gain

    outs = (q_ref, k_ref, v_ref, xl_ref, gl_ref)
    for idx, o_ref in enumerate(outs):
        z = jnp.dot(n, w_ref[:, idx * D_ATT:(idx + 1) * D_ATT], preferred_element_type=F32)
        if idx < 2:
            z = head_norm(z, qkg_ref[idx:idx + 1, :])
        for p in range(N_GROUPS):
            o_ref[p] = z[:, p * LANES:(p + 1) * LANES].astype(BF16)


def _in_projection(x, mod, ln1_g, w_in_bf, gmat, qkg, tm):
    bsz, t, _ = x.shape
    grouped = jax.ShapeDtypeStruct((bsz, N_GROUPS, t, LANES), BF16)
    out_spec = pl.BlockSpec((None, N_GROUPS, tm, LANES), lambda b, i: (b, 0, i, 0))
    return pl.pallas_call(
        _inproj_kernel,
        grid=(bsz, t // tm),
        in_specs=[
            pl.BlockSpec((None, tm, D_MODEL), lambda b, i: (b, i, 0)),
            pl.BlockSpec((None, SUBLANES, D_MODEL), lambda b, i: (b, 0, 0)),
            _const_spec((1, D_MODEL)),
            _const_spec((D_MODEL, D_IN)),
            _const_spec((MXU_WIDTH, MXU_WIDTH)),
            _const_spec((SUBLANES, D_ATT)),
        ],
        out_specs=[out_spec] * 5,
        out_shape=[grouped] * 5,
        compiler_params=pltpu.CompilerParams(
            dimension_semantics=("parallel", "parallel"), vmem_limit_bytes=VMEM_LIMIT),
        name="in_projection",
    )(x, mod, ln1_g, w_in_bf, gmat, qkg)


def _attn_kernel(q_ref, k_ref, v_ref, bias_ref, o_ref, *, rows):
    lane = lax.broadcasted_iota(jnp.int32, (GRID_W, LANES), 1)
    low = lane < HEAD_DIM

    def scores(r):
        rs = jnp.clip(r - NA_ROWS // 2, 0, rows - NA_ROWS)
        q0 = pl.multiple_of(r * GRID_W, GRID_W)
        k0 = pl.multiple_of(rs * GRID_W, GRID_W)
        q2 = q_ref[pl.ds(q0, GRID_W), :].astype(F32)
        qq = jnp.concatenate([jnp.where(low, q2, 0.0), jnp.where(low, 0.0, q2)], axis=0).astype(BF16)
        k2 = k_ref[pl.ds(k0, WIN_KEYS), :]
        s = lax.dot_general(qq, k2, (((1,), (1,)), ((), ())), preferred_element_type=F32)
        return s + bias_ref[r - rs], q0, k0

    def softmax(s):
        m = jnp.max(s, axis=-1, keepdims=True)
        e = jnp.exp2(s - m)
        return e.astype(BF16), jnp.sum(e, axis=-1, keepdims=True)

    def weighted(e, l, q0, k0):
        v2 = v_ref[pl.ds(k0, WIN_KEYS), :]
        pv = jnp.dot(e, v2, preferred_element_type=F32)
        pv = pv * (1.0 / l)
        out = jnp.where(low, pv[:GRID_W], pv[GRID_W:])
        o_ref[pl.ds(q0, GRID_W), :] = out.astype(BF16)

    def group_step(g, carry):
        staged = [scores(g * ROW_UNROLL + i) for i in range(ROW_UNROLL)]
        probs = [softmax(s) for s, _, _ in staged]
        for (e, l), (_, q0, k0) in zip(probs, staged):
            weighted(e, l, q0, k0)
        return carry

    lax.fori_loop(0, rows // ROW_UNROLL, group_step, 0)


def _attention(q, k, v, bias):
    bsz, _, t, _ = q.shape
    rows = t // GRID_W
    seq_spec = pl.BlockSpec((None, None, t, LANES), lambda p, b: (b, p, 0, 0))
    return pl.pallas_call(
        functools.partial(_attn_kernel, rows=rows),
        grid=(N_GROUPS, bsz),
        in_specs=[seq_spec, seq_spec, seq_spec,
                  pl.BlockSpec((None, NA_ROWS, 2 * GRID_W, WIN_KEYS), lambda p, b: (p, 0, 0, 0))],
        out_specs=seq_spec,
        out_shape=jax.ShapeDtypeStruct(q.shape, BF16),
        compiler_params=pltpu.CompilerParams(
            dimension_semantics=("parallel", "parallel"), vmem_limit_bytes=VMEM_LIMIT),
        name="nbr_attention",
    )(q, k, v, bias)


def _attention_bias(rpb):
    cols = np.arange(GRID_W)
    col_start = np.clip(cols - NA_COLS // 2, 0, GRID_W - NA_COLS)
    in_win = (cols[None, :] >= col_start[:, None]) & (cols[None, :] < col_start[:, None] + NA_COLS)
    dc = cols[None, :] - cols[:, None] + (NA_COLS - 1)
    onehot = (np.arange(2 * NA_COLS - 1)[:, None, None] == dc[None]) & in_win[None]
    tab = jnp.einsum("hrd,dcx->hrcx", rpb.astype(F32), jnp.asarray(onehot, F32),
                     precision=lax.Precision.HIGHEST)
    tab = jnp.where(jnp.asarray(in_win)[None, None], tab * LOG2_E, MASK_VALUE)
    b = jnp.stack([tab[:, NA_ROWS - 1 - o:2 * NA_ROWS - 1 - o] for o in range(NA_ROWS)], axis=1)
    b = jnp.transpose(b, (0, 1, 3, 2, 4)).reshape(ATT_HEADS, NA_ROWS, GRID_W, WIN_KEYS)
    b = b.reshape(N_GROUPS, 2, NA_ROWS, GRID_W, WIN_KEYS)
    return jnp.transpose(b, (0, 2, 1, 3, 4)).reshape(N_GROUPS, NA_ROWS, 2 * GRID_W, WIN_KEYS)


def _lru_kernel(xl_ref, gl_ref, cw_ref, wf_ref, wb_ref, par_ref, o_ref,
                xp_s, xc_s, hf_s, qf_s, hb_s, qb_s, *, t, tc):
    seg = t // SUBLANES
    pitch = seg + LRU_SEG_PAD
    n_conv = t // tc
    conv_per_seg = seg // tc
    n_steps = seg // LRU_STEPS
    zeros8 = jnp.zeros((SUBLANES, LANES), F32)
    ones8 = jnp.ones((SUBLANES, LANES), F32)
    xp_s[0:SUBLANES, :] = zeros8
    xp_s[SUBLANES + t:, :] = zeros8
    xp_s[SUBLANES:SUBLANES + t, :] = xl_ref[...].astype(F32)

    def softplus(z):
        return jnp.maximum(z, 0.0) + jnp.log(1.0 + jnp.exp(-jnp.abs(z)))

    half_decay = [(-0.5 * LRU_C) * softplus(-par_ref[row:row + 1, :]) for row in (2, 5)]

    def gates(xc, w_ref, direction):
        row0 = 3 * direction
        g = jnp.dot(xc.astype(BF16), w_ref[...], preferred_element_type=F32)
        tr = jnp.tanh(g[:, :LANES] + par_ref[row0:row0 + 1, :])
        ti = jnp.tanh(g[:, LANES:] + par_ref[row0 + 1:row0 + 2, :])
        log_a = half_decay[direction] * tr + half_decay[direction]
        a = jnp.exp(log_a)
        beta_sq = (a * a + 1.0) * (-jnp.tanh(log_a))
        beta = beta_sq * lax.rsqrt(jnp.maximum(beta_sq, TINY))
        u = beta * ((0.5 * ti + 0.5) * xc)
        return a, u

    for c in range(n_conv):
        lo = SUBLANES + c * tc
        acc = cw_ref[1:2, :] * xp_s[lo:lo + tc, :] + cw_ref[4:5, :]
        for tap in (0, 2, 3):
            acc = acc + cw_ref[tap:tap + 1, :] * xp_s[lo + tap - 1:lo + tap - 1 + tc, :]
        r0 = (c // conv_per_seg) * pitch + (c % conv_per_seg) * tc
        xc_s[r0:r0 + tc, :] = acc

    def tile(ref, k):
        return ref.at[pl.ds(k, SUBLANES, stride=pitch), :]

    def gather(k0):
        return jnp.concatenate([tile(xc_s, k0 + kk)[...] for kk in range(LRU_STEPS)], axis=0)

    def scan_step(c, carry):
        hf, qf, hb, qb = carry
        kf = c * LRU_STEPS
        kb = (n_steps - 1 - c) * LRU_STEPS
        af, uf = gates(gather(kf), wf_ref, 0)
        ab, ub = gates(gather(kb), wb_ref, 1)
        for kk in range(LRU_STEPS):
            rows = slice(kk * SUBLANES, (kk + 1) * SUBLANES)
            hf = af[rows] * hf + uf[rows]
            qf = af[rows] * qf
            tile(hf_s, kf + kk)[...] = hf
            tile(qf_s, kf + kk)[...] = qf
            kr = LRU_STEPS - 1 - kk
            rows = slice(kr * SUBLANES, (kr + 1) * SUBLANES)
            hb = ab[rows] * hb + ub[rows]
            qb = ab[rows] * qb
            tile(hb_s, kb + kr)[...] = hb
            tile(qb_s, kb + kr)[...] = qb
        return hf, qf, hb, qb

    hf, qf, hb, qb = lax.fori_loop(0, n_steps, scan_step, (zeros8, ones8, zeros8, ones8))

    cf = [jnp.zeros((1, LANES), F32)]
    for s in range(SUBLANES - 1):
        cf.append(qf[s:s + 1] * cf[s] + hf[s:s + 1])
    cb = [jnp.zeros((1, LANES), F32)]
    for s in range(SUBLANES - 1, 0, -1):
        cb.insert(0, qb[s:s + 1] * cb[0] + hb[s:s + 1])

    for c in range(n_conv):
        s = c // conv_per_seg
        r0 = s * pitch + (c % conv_per_seg) * tc
        h = (hf_s[r0:r0 + tc, :] + qf_s[r0:r0 + tc, :] * cf[s]) + (hb_s[r0:r0 + tc, :] + qb_s[r0:r0 + tc, :] * cb[s])
        gate = _gelu_tanh(gl_ref[c * tc:(c + 1) * tc, :].astype(F32))
        o_ref[c * tc:(c + 1) * tc, :] = (h * gate).astype(BF16)


def _rg_lru(xl, gl, cw, wf, wb, par, tc):
    bsz, _, t, _ = xl.shape
    seg = t // SUBLANES
    tc = min(tc, seg)
    assert seg % tc == 0 and seg % LRU_STEPS == 0
    pitched = SUBLANES * (seg + LRU_SEG_PAD)
    seq_spec = pl.BlockSpec((None, None, t, LANES), lambda b, g: (b, g, 0, 0))

    def group_spec(shape):
        return pl.BlockSpec((None,) + shape, lambda b, g: (g,) + (0,) * len(shape))

    return pl.pallas_call(
        functools.partial(_lru_kernel, t=t, tc=tc),
        grid=(bsz, N_GROUPS),
        in_specs=[seq_spec, seq_spec, group_spec((SUBLANES, LANES)), group_spec((LANES, 2 * LANES)),
                  group_spec((LANES, 2 * LANES)), group_spec((SUBLANES, LANES))],
        out_specs=seq_spec,
        out_shape=jax.ShapeDtypeStruct(xl.shape, BF16),
        scratch_shapes=[pltpu.VMEM((t + 2 * SUBLANES, LANES), F32)] + [pltpu.VMEM((pitched, LANES), F32)] * 5,
        compiler_params=pltpu.CompilerParams(
            dimension_semantics=("parallel", "parallel"), vmem_limit_bytes=VMEM_LIMIT),
        name="rg_lru",
    )(xl, gl, cw, wf, wb, par)


def _block_diag_pairs(w):
    w = w.reshape(N_GROUPS, 2, LRU_BLOCK, LRU_BLOCK)
    z = jnp.zeros((N_GROUPS, LRU_BLOCK, LRU_BLOCK), w.dtype)
    top = jnp.concatenate([w[:, 0], z], axis=-1)
    bot = jnp.concatenate([z, w[:, 1]], axis=-1)
    return jnp.concatenate([top, bot], axis=-2)


def _mlp_kernel(x_ref, xp_ref, xn_ref, a_ref, ap_ref, an_ref, l_ref, lp_ref, ln_ref, mod_ref, ln2_ref,
                wout_ref, wup_ref, cw_ref, wdn_ref, o_ref, al_s, x1_s, n2_s, acc_s, u_s, *, tm, t):
    i = pl.program_id(1)
    n_ext = tm + 2 * HALO
    for src, lane0 in (((ap_ref, a_ref, an_ref), 0), ((lp_ref, l_ref, ln_ref), D_ATT)):
        prev_ref, main_ref, next_ref = src
        for p in range(N_GROUPS):
            lanes = slice(lane0 + p * LANES, lane0 + (p + 1) * LANES)
            al_s[0:HALO, lanes] = prev_ref[p]
            al_s[HALO:HALO + tm, lanes] = main_ref[p]
            al_s[HALO + tm:, lanes] = next_ref[p]
    x1_s[0:HALO, :] = xp_ref[...]
    x1_s[HALO:HALO + tm, :] = x_ref[...]
    x1_s[HALO + tm:, :] = xn_ref[...]

    rows_per_block = n_ext // PROLOGUE_BLOCKS
    assert rows_per_block * PROLOGUE_BLOCKS == n_ext and rows_per_block % HALO == 0
    for blk in range(PROLOGUE_BLOCKS):
        r0, r1 = blk * rows_per_block, (blk + 1) * rows_per_block
        mix = jnp.dot(al_s[r0:r1, :], wout_ref[...], preferred_element_type=F32)
        x1 = x1_s[r0:r1, :] + mod_ref[2:3, :] * mix
        x1_s[r0:r1, :] = x1
        ms = jnp.mean(x1 * x1, axis=-1, keepdims=True)
        n2 = x1 * lax.rsqrt(ms + NORM_EPS) * ln2_ref[...] * (1.0 + mod_ref[4:5, :]) + mod_ref[3:4, :]
        tok = i * tm - HALO + r0 + lax.broadcasted_iota(jnp.int32, (r1 - r0, 1), 0)
        n2_s[r0:r1, :] = jnp.where((tok >= 0) & (tok < t), n2, 0.0).astype(BF16)

    def up(c, slot):
        c0, c1 = FF_CHUNKS[c]
        n2 = n2_s[...]
        u_s[slot, 0, :, 0:c1 - c0] = jnp.dot(n2, wup_ref[:, c0:c1], preferred_element_type=F32)
        u_s[slot, 1, :, 0:c1 - c0] = jnp.dot(n2, wup_ref[:, D_FF + c0:D_FF + c1], preferred_element_type=F32)

    def conv(slot, half, c0, c1):
        w = c1 - c0
        cw = cw_ref[:, half * D_FF + c0:half * D_FF + c1]
        c = cw[1:2, :] * u_s[slot, half, HALO:HALO + tm, 0:w] + cw[3:4, :]
        c = c + cw[0:1, :] * u_s[slot, half, HALO - 1:HALO - 1 + tm, 0:w]
        return c + cw[2:3, :] * u_s[slot, half, HALO + 1:HALO + 1 + tm, 0:w]

    def down(c, slot):
        c0, c1 = FF_CHUNKS[c]
        ug = conv(slot, 0, c0, c1)
        uv = conv(slot, 1, c0, c1)
        act = (ug * (1.0 + _gelu_inner_tanh(ug)) * uv).astype(BF16)
        d = jnp.dot(act, wdn_ref[c0:c1, :], preferred_element_type=F32)
        acc_s[...] = d if c == 0 else acc_s[...] + d

    up(0, 0)
    for c in range(len(FF_CHUNKS)):
        if c + 1 < len(FF_CHUNKS):
            up(c + 1, (c + 1) % 2)
        down(c, c % 2)
    o_ref[...] = x1_s[HALO:HALO + tm, :] + mod_ref[5:6, :] * acc_s[...]


def _out_mlp(x, att, lru, mod, ln2_g, w_out_bf, w_up_bf, ffn_cw, w_dn_bf, tm):
    bsz, t, _ = x.shape
    per_halo = tm // HALO
    n_halo = t // HALO

    def main_map(b, i):
        return (b, i, 0)

    def prev_map(b, i):
        return (b, jnp.maximum(i * per_halo - 1, 0), 0)

    def next_map(b, i):
        return (b, jnp.minimum((i + 1) * per_halo, n_halo - 1), 0)

    def grouped(rows, imap):
        return pl.BlockSpec((None, N_GROUPS, rows, LANES), lambda b, i: (imap(b, i)[0], 0, imap(b, i)[1], 0))

    n_ext = tm + 2 * HALO
    return pl.pallas_call(
        functools.partial(_mlp_kernel, tm=tm, t=t),
        grid=(bsz, t // tm),
        in_specs=[
            pl.BlockSpec((None, tm, D_MODEL), main_map),
            pl.BlockSpec((None, HALO, D_MODEL), prev_map),
            pl.BlockSpec((None, HALO, D_MODEL), next_map),
            grouped(tm, main_map), grouped(HALO, prev_map), grouped(HALO, next_map),
            grouped(tm, main_map), grouped(HALO, prev_map), grouped(HALO, next_map),
            pl.BlockSpec((None, SUBLANES, D_MODEL), lambda b, i: (b, 0, 0)),
            _const_spec((1, D_MODEL)),
            _const_spec((D_MODEL, D_MODEL)),
            _const_spec((D_MODEL, 2 * D_FF)),
            _const_spec((SUBLANES, 2 * D_FF)),
            _const_spec((D_FF, D_MODEL)),
        ],
        out_specs=pl.BlockSpec((None, tm, D_MODEL), main_map),
        out_shape=jax.ShapeDtypeStruct(x.shape, F32),
        scratch_shapes=[pltpu.VMEM((n_ext, D_MODEL), BF16),
                        pltpu.VMEM((n_ext, D_MODEL), F32),
                        pltpu.VMEM((n_ext, D_MODEL), BF16),
                        pltpu.VMEM((tm, D_MODEL), F32),
                        pltpu.VMEM((2, 2, n_ext, FF_CHUNK_MAX), F32)],
        compiler_params=pltpu.CompilerParams(
            dimension_semantics=("parallel", "parallel"), vmem_limit_bytes=VMEM_LIMIT),
        name="out_mlp",
    )(x, x, x, att, att, att, lru, lru, lru, mod, ln2_g, w_out_bf, w_up_bf, ffn_cw, w_dn_bf)


def _pad_rows(a, rows):
    return jnp.concatenate([a, jnp.zeros((rows - a.shape[0],) + a.shape[1:], a.dtype)], axis=0)


def _prepare_params(ln1_g, ln2_g, w_in, q_norm_g, k_norm_g, rpb, lru_conv_w, lru_conv_b,
                    w_r_f, b_r_f, w_i_f, b_i_f, lam_f, w_r_b, b_r_b, w_i_b, b_i_b, lam_b,
                    w_out, w_up, ffn_conv_w, ffn_conv_b, w_down):
    head = jnp.arange(MXU_WIDTH) // HEAD_DIM
    gmat = (head[:, None] == head[None, :]).astype(BF16)
    scale = HEAD_DIM ** -0.5 * LOG2_E
    qkg = _pad_rows(jnp.stack([jnp.tile(q_norm_g.astype(F32), ATT_HEADS) * scale,
                               jnp.tile(k_norm_g.astype(F32), ATT_HEADS)]), SUBLANES)
    lru_cw = _pad_rows(jnp.concatenate([lru_conv_w, lru_conv_b[None]], axis=0), SUBLANES)
    lru_cw = jnp.transpose(lru_cw.reshape(SUBLANES, N_GROUPS, LANES), (1, 0, 2))
    wf = (0.5 * jnp.concatenate([_block_diag_pairs(w_r_f), _block_diag_pairs(w_i_f)], axis=-1)).astype(BF16)
    wb = (0.5 * jnp.concatenate([_block_diag_pairs(w_r_b), _block_diag_pairs(w_i_b)], axis=-1)).astype(BF16)
    par = _pad_rows(jnp.stack([0.5 * b_r_f, 0.5 * b_i_f, lam_f, 0.5 * b_r_b, 0.5 * b_i_b, lam_b]), SUBLANES)
    par = jnp.transpose(par.reshape(SUBLANES, N_GROUPS, LANES), (1, 0, 2))
    ffn_cw = _pad_rows(jnp.concatenate([ffn_conv_w, ffn_conv_b[None]], axis=0), SUBLANES)
    ffn_cw = ffn_cw * jnp.where(jnp.arange(2 * D_FF) < D_FF, 1.0, 0.5)[None, :]
    return dict(
        ln1_g=ln1_g.reshape(1, D_MODEL), ln2_g=ln2_g.reshape(1, D_MODEL),
        w_in=w_in.astype(BF16), gmat=gmat, qkg=qkg, bias=_attention_bias(rpb),
        lru_cw=lru_cw, wf=wf, wb=wb, par=par,
        w_out=w_out.astype(BF16), w_up=w_up.astype(BF16), ffn_cw=ffn_cw, w_down=w_down.astype(BF16))


def _encoder_layer(x, mod, p, tm, tc):
    q, k, v, xl, gl = _in_projection(x, mod, p["ln1_g"], p["w_in"], p["gmat"], p["qkg"], tm)
    att = _attention(q, k, v, p["bias"])
    lru = _rg_lru(xl, gl, p["lru_cw"], p["wf"], p["wb"], p["par"], tc)
    return _out_mlp(x, att, lru, mod, p["ln2_g"], p["w_out"], p["w_up"], p["ffn_cw"], p["w_down"], tm)


def _forward(x_prompt, x_sample, c_prompt, c_sample, w_ada, b_ada, layer_params, tm=512, tc=512):
    nb_p, nb_s = c_prompt.shape[0], c_sample.shape[0]
    depth = w_ada.shape[0]
    for layer in range(depth):
        c_all = jnp.concatenate([c_prompt, c_sample], axis=0)
        c_all = _pad_rows(c_all, -(-c_all.shape[0] // SUBLANES) * SUBLANES)
        mod = _modulation(c_all, w_ada[layer], b_ada[layer])
        mod = mod.reshape(mod.shape[0], N_MOD, D_MODEL)
        mod = jnp.concatenate([mod, jnp.zeros((mod.shape[0], SUBLANES - N_MOD, D_MODEL), F32)], axis=1)
        p = _prepare_params(*[w[layer] for w in layer_params])
        x_prompt = _encoder_layer(x_prompt, mod[:nb_p], p, tm, tc)
        x_sample = _encoder_layer(x_sample, mod[nb_p:nb_p + nb_s], p, tm, tc)
    return x_prompt, x_sample


def kernel(x_prompt, x_sample, c_prompt, c_sample, ln1_g, ln2_g, w_ada, b_ada, w_in, q_norm_g, k_norm_g, rpb,
           lru_conv_w, lru_conv_b, w_r_f, b_r_f, w_i_f, b_i_f, lam_f, w_r_b, b_r_b, w_i_b, b_i_b, lam_b,
           w_out, w_up, ffn_conv_w, ffn_conv_b, w_down):
    layer_params = (ln1_g, ln2_g, w_in, q_norm_g, k_norm_g, rpb, lru_conv_w, lru_conv_b,
                    w_r_f, b_r_f, w_i_f, b_i_f, lam_f, w_r_b, b_r_b, w_i_b, b_i_b, lam_b,
                    w_out, w_up, ffn_conv_w, ffn_conv_b, w_down)
    return _forward(x_prompt, x_sample, c_prompt, c_sample, w_ada, b_ada, layer_params)
```

```python
import functools
import math

import numpy as np

import jax
import jax.numpy as jnp
from jax import lax
from jax.experimental import pallas as pl
from jax.experimental.pallas import tpu as pltpu

D_MODEL = 1024
GRID_W = 64
ATT_HEADS = 8
HEAD_DIM = 64
D_ATT = ATT_HEADS * HEAD_DIM
D_LRU = 512
LRU_HEADS = 8
LRU_BLOCK = D_LRU // LRU_HEADS
NA_ROWS = 8
NA_COLS = 16
LRU_CONV = 4
LRU_C = 8.0
FFN_CONV = 3
D_FF = 2816
D_IN = 3 * D_ATT + 2 * D_LRU
N_MOD = 6
NORM_EPS = 1e-6

LANES = 128
SUBLANES = 8
MXU_WIDTH = 256
N_GROUPS = D_ATT // LANES
WIN_KEYS = NA_ROWS * GRID_W
FF_CHUNKS = ((0, 6 * MXU_WIDTH), (6 * MXU_WIDTH, D_FF))
FF_CHUNK_MAX = max(c1 - c0 for c0, c1 in FF_CHUNKS)
HALO = 16
PROLOGUE_BLOCKS = 2
ROW_UNROLL = 8
LRU_STEPS = 256
LRU_SEG_PAD = 8
MASK_VALUE = -1e30
LOG2_E = math.log2(math.e)
TINY = 1e-30
VMEM_LIMIT = 56 * 1024 * 1024

BF16 = jnp.bfloat16
F32 = jnp.float32


def _gelu_tanh(x):
    return 0.5 * x * (1.0 + jnp.tanh(math.sqrt(2.0 / math.pi) * (x + 0.044715 * (x * x * x))))


def _gelu_inner_tanh(x):
    c = math.sqrt(2.0 / math.pi)
    return jnp.tanh(x * (c + (0.044715 * c) * (x * x)))


def _sigmoid(x):
    return 0.5 * jnp.tanh(0.5 * x) + 0.5


def _const_spec(shape):
    zeros = (0,) * len(shape)
    return pl.BlockSpec(shape, lambda *_: zeros, pipeline_mode=pl.Buffered(1))


def _mod_kernel(c_ref, w_ref, b_ref, o_ref):
    c = c_ref[...]
    s = (c * _sigmoid(c)).astype(BF16)
    o_ref[...] = jnp.dot(s, w_ref[...].astype(BF16), preferred_element_type=F32) + b_ref[...]


def _modulation(c_all, w_ada, b_ada):
    bp = c_all.shape[0]
    return pl.pallas_call(
        _mod_kernel,
        grid=(N_MOD,),
        in_specs=[
            pl.BlockSpec((bp, D_MODEL), lambda j: (0, 0)),
            pl.BlockSpec((D_MODEL, D_MODEL), lambda j: (0, j)),
            pl.BlockSpec((1, D_MODEL), lambda j: (0, j)),
        ],
        out_specs=pl.BlockSpec((bp, D_MODEL), lambda j: (0, j)),
        out_shape=jax.ShapeDtypeStruct((bp, N_MOD * D_MODEL), F32),
        compiler_params=pltpu.CompilerParams(dimension_semantics=("arbitrary",), vmem_limit_bytes=VMEM_LIMIT),
        name="modulation",
    )(c_all, w_ada, b_ada.reshape(1, -1))


def _inproj_kernel(x_ref, mod_ref, ln_ref, w_ref, gmat_ref, qkg_ref,
                   q_ref, k_ref, v_ref, xl_ref, gl_ref):
    x = x_ref[...]
    ms = jnp.mean(x * x, axis=-1, keepdims=True)
    y = x * lax.rsqrt(ms + NORM_EPS) * ln_ref[...]
    n = (y * (1.0 + mod_ref[1:2, :]) + mod_ref[0:1, :]).astype(BF16)

    def head_norm(z, gain):
        zz = (z * z).astype(BF16)
        width = gmat_ref.shape[0]
        ss = jnp.concatenate(
            [jnp.dot(zz[:, c:c + width], gmat_ref[...], preferred_element_type=F32) for c in range(0, D_ATT, width)],
            axis=1)
        return z * lax.rsqrt(ss * (1.0 / HEAD_DIM) + NORM_EPS) * gain

    outs = (q_ref, k_ref, v_ref, xl_ref, gl_ref)
    for idx, o_ref in enumerate(outs):
        z = jnp.dot(n, w_ref[:, idx * D_ATT:(idx + 1) * D_ATT], preferred_element_type=F32)
        if idx < 2:
            z = head_norm(z, qkg_ref[idx:idx + 1, :])
        for p in range(N_GROUPS):
            o_ref[p] = z[:, p * LANES:(p + 1) * LANES].astype(BF16)


def _in_projection(x, mod, ln1_g, w_in_bf, gmat, qkg, tm):
    bsz, t, _ = x.shape
    grouped = jax.ShapeDtypeStruct((bsz, N_GROUPS, t, LANES), BF16)
    out_spec = pl.BlockSpec((None, N_GROUPS, tm, LANES), lambda b, i: (b, 0, i, 0))
    return pl.pallas_call(
        _inproj_kernel,
        grid=(bsz, t // tm),
        in_specs=[
            pl.BlockSpec((None, tm, D_MODEL), lambda b, i: (b, i, 0)),
            pl.BlockSpec((None, SUBLANES, D_MODEL), lambda b, i: (b, 0, 0)),
            _const_spec((1, D_MODEL)),
            _const_spec((D_MODEL, D_IN)),
            _const_spec((MXU_WIDTH, MXU_WIDTH)),
            _const_spec((SUBLANES, D_ATT)),
        ],
        out_specs=[out_spec] * 5,
        out_shape=[grouped] * 5,
        compiler_params=pltpu.CompilerParams(
            dimension_semantics=("parallel", "parallel"), vmem_limit_bytes=VMEM_LIMIT),
        name="in_projection",
    )(x, mod, ln1_g, w_in_bf, gmat, qkg)


def _attn_kernel(q_ref, k_ref, v_ref, bias_ref, o_ref, *, rows):
    lane = lax.broadcasted_iota(jnp.int32, (GRID_W, LANES), 1)
    low = lane < HEAD_DIM

    def scores(r):
        rs = jnp.clip(r - NA_ROWS // 2, 0, rows - NA_ROWS)
        q0 = pl.multiple_of(r * GRID_W, GRID_W)
        k0 = pl.multiple_of(rs * GRID_W, GRID_W)
        q2 = q_ref[pl.ds(q0, GRID_W), :].astype(F32)
        qq = jnp.concatenate([jnp.where(low, q2, 0.0), jnp.where(low, 0.0, q2)], axis=0).astype(BF16)
        k2 = k_ref[pl.ds(k0, WIN_KEYS), :]
        s = lax.dot_general(qq, k2, (((1,), (1,)), ((), ())), preferred_element_type=F32)
        return s + bias_ref[r - rs], q0, k0

    def softmax(s):
        m = jnp.max(s, axis=-1, keepdims=True)
        return jnp.exp2(s - m).astype(BF16)

    ones = jnp.ones((WIN_KEYS, LANES), BF16)

    def weighted(e, q0, k0):
        v2 = jnp.concatenate([v_ref[pl.ds(k0, WIN_KEYS), :], ones], axis=1)
        pv = jnp.dot(e, v2, preferred_element_type=F32)
        pv = pv[:, :LANES] * (1.0 / pv[:, LANES:])
        out = jnp.where(low, pv[:GRID_W], pv[GRID_W:])
        o_ref[pl.ds(q0, GRID_W), :] = out.astype(BF16)

    def group_step(g, carry):
        staged = [scores(g * ROW_UNROLL + i) for i in range(ROW_UNROLL)]
        probs = [softmax(s) for s, _, _ in staged]
        for e, (_, q0, k0) in zip(probs, staged):
            weighted(e, q0, k0)
        return carry

    lax.fori_loop(0, rows // ROW_UNROLL, group_step, 0)


def _attention(q, k, v, bias):
    bsz, _, t, _ = q.shape
    rows = t // GRID_W
    seq_spec = pl.BlockSpec((None, None, t, LANES), lambda p, b: (b, p, 0, 0))
    return pl.pallas_call(
        functools.partial(_attn_kernel, rows=rows),
        grid=(N_GROUPS, bsz),
        in_specs=[seq_spec, seq_spec, seq_spec,
                  pl.BlockSpec((None, NA_ROWS, 2 * GRID_W, WIN_KEYS), lambda p, b: (p, 0, 0, 0))],
        out_specs=seq_spec,
        out_shape=jax.ShapeDtypeStruct(q.shape, BF16),
        compiler_params=pltpu.CompilerParams(
            dimension_semantics=("parallel", "parallel"), vmem_limit_bytes=VMEM_LIMIT),
        name="nbr_attention",
    )(q, k, v, bias)


def _attention_bias(rpb):
    cols = np.arange(GRID_W)
    col_start = np.clip(cols - NA_COLS // 2, 0, GRID_W - NA_COLS)
    in_win = (cols[None, :] >= col_start[:, None]) & (cols[None, :] < col_start[:, None] + NA_COLS)
    dc = cols[None, :] - cols[:, None] + (NA_COLS - 1)
    onehot = (np.arange(2 * NA_COLS - 1)[:, None, None] == dc[None]) & in_win[None]
    tab = jnp.einsum("hrd,dcx->hrcx", rpb.astype(F32), jnp.asarray(onehot, F32),
                     precision=lax.Precision.HIGHEST)
    tab = jnp.where(jnp.asarray(in_win)[None, None], tab * LOG2_E, MASK_VALUE)
    b = jnp.stack([tab[:, NA_ROWS - 1 - o:2 * NA_ROWS - 1 - o] for o in range(NA_ROWS)], axis=1)
    b = jnp.transpose(b, (0, 1, 3, 2, 4)).reshape(ATT_HEADS, NA_ROWS, GRID_W, WIN_KEYS)
    b = b.reshape(N_GROUPS, 2, NA_ROWS, GRID_W, WIN_KEYS)
    return jnp.transpose(b, (0, 2, 1, 3, 4)).reshape(N_GROUPS, NA_ROWS, 2 * GRID_W, WIN_KEYS)


def _lru_kernel(xl_ref, gl_ref, cw_ref, wf_ref, wb_ref, par_ref, o_ref,
                xp_s, xc_s, hf_s, qf_s, hb_s, qb_s, *, t, tc):
    seg = t // SUBLANES
    pitch = seg + LRU_SEG_PAD
    n_conv = t // tc
    conv_per_seg = seg // tc
    n_steps = seg // LRU_STEPS
    zeros8 = jnp.zeros((SUBLANES, LANES), F32)
    ones8 = jnp.ones((SUBLANES, LANES), F32)
    xp_s[0:SUBLANES, :] = zeros8
    xp_s[SUBLANES + t:, :] = zeros8
    xp_s[SUBLANES:SUBLANES + t, :] = xl_ref[...].astype(F32)

    def softplus(z):
        return jnp.maximum(z, 0.0) + jnp.log(1.0 + jnp.exp(-jnp.abs(z)))

    half_decay = [(-0.5 * LRU_C) * softplus(-par_ref[row:row + 1, :]) for row in (2, 5)]

    def gates(xc, w_ref, direction):
        row0 = 3 * direction
        g = jnp.dot(xc.astype(BF16), w_ref[...], preferred_element_type=F32)
        tr = jnp.tanh(g[:, :LANES] + par_ref[row0:row0 + 1, :])
        ti = jnp.tanh(g[:, LANES:] + par_ref[row0 + 1:row0 + 2, :])
        log_a = half_decay[direction] * tr + half_decay[direction]
        a = jnp.exp(log_a)
        beta_sq = (a * a + 1.0) * (-jnp.tanh(log_a))
        beta = beta_sq * lax.rsqrt(jnp.maximum(beta_sq, TINY))
        u = beta * ((0.5 * ti + 0.5) * xc)
        return a, u

    for c in range(n_conv):
        lo = SUBLANES + c * tc
        acc = cw_ref[1:2, :] * xp_s[lo:lo + tc, :] + cw_ref[4:5, :]
        for tap in (0, 2, 3):
            acc = acc + cw_ref[tap:tap + 1, :] * xp_s[lo + tap - 1:lo + tap - 1 + tc, :]
        r0 = (c // conv_per_seg) * pitch + (c % conv_per_seg) * tc
        xc_s[r0:r0 + tc, :] = acc

    def tile(ref, k):
        return ref.at[pl.ds(k, SUBLANES, stride=pitch), :]

    def gather(k0):
        return jnp.concatenate([tile(xc_s, k0 + kk)[...] for kk in range(LRU_STEPS)], axis=0)

    def scan_step(c, carry):
        hf, qf, hb, qb = carry
        kf = c * LRU_STEPS
        kb = (n_steps - 1 - c) * LRU_STEPS
        af, uf = gates(gather(kf), wf_ref, 0)
        ab, ub = gates(gather(kb), wb_ref, 1)
        for kk in range(LRU_STEPS):
            rows = slice(kk * SUBLANES, (kk + 1) * SUBLANES)
            hf = af[rows] * hf + uf[rows]
            qf = af[rows] * qf
            tile(hf_s, kf + kk)[...] = hf
            tile(qf_s, kf + kk)[...] = qf
            kr = LRU_STEPS - 1 - kk
            rows = slice(kr * SUBLANES, (kr + 1) * SUBLANES)
            hb = ab[rows] * hb + ub[rows]
            qb = ab[rows] * qb
            tile(hb_s, kb + kr)[...] = hb
            tile(qb_s, kb + kr)[...] = qb
        return hf, qf, hb, qb

    hf, qf, hb, qb = lax.fori_loop(0, n_steps, scan_step, (zeros8, ones8, zeros8, ones8))

    cf = [jnp.zeros((1, LANES), F32)]
    for s in range(SUBLANES - 1):
        cf.append(qf[s:s + 1] * cf[s] + hf[s:s + 1])
    cb = [jnp.zeros((1, LANES), F32)]
    for s in range(SUBLANES - 1, 0, -1):
        cb.insert(0, qb[s:s + 1] * cb[0] + hb[s:s + 1])

    for c in range(n_conv):
        s = c // conv_per_seg
        r0 = s * pitch + (c % conv_per_seg) * tc
        h = (hf_s[r0:r0 + tc, :] + qf_s[r0:r0 + tc, :] * cf[s]) + (hb_s[r0:r0 + tc, :] + qb_s[r0:r0 + tc, :] * cb[s])
        gate = _gelu_tanh(gl_ref[c * tc:(c + 1) * tc, :].astype(F32))
        o_ref[c * tc:(c + 1) * tc, :] = (h * gate).astype(BF16)


def _rg_lru(xl, gl, cw, wf, wb, par, tc):
    bsz, _, t, _ = xl.shape
    seg = t // SUBLANES
    tc = min(tc, seg)
    assert seg % tc == 0 and seg % LRU_STEPS == 0
    pitched = SUBLANES * (seg + LRU_SEG_PAD)
    seq_spec = pl.BlockSpec((None, None, t, LANES), lambda b, g: (b, g, 0, 0))

    def group_spec(shape):
        return pl.BlockSpec((None,) + shape, lambda b, g: (g,) + (0,) * len(shape))

    return pl.pallas_call(
        functools.partial(_lru_kernel, t=t, tc=tc),
        grid=(bsz, N_GROUPS),
        in_specs=[seq_spec, seq_spec, group_spec((SUBLANES, LANES)), group_spec((LANES, 2 * LANES)),
                  group_spec((LANES, 2 * LANES)), group_spec((SUBLANES, LANES))],
        out_specs=seq_spec,
        out_shape=jax.ShapeDtypeStruct(xl.shape, BF16),
        scratch_shapes=[pltpu.VMEM((t + 2 * SUBLANES, LANES), F32)] + [pltpu.VMEM((pitched, LANES), F32)] * 5,
        compiler_params=pltpu.CompilerParams(
            dimension_semantics=("parallel", "parallel"), vmem_limit_bytes=VMEM_LIMIT),
        name="rg_lru",
    )(xl, gl, cw, wf, wb, par)


def _block_diag_pairs(w):
    w = w.reshape(N_GROUPS, 2, LRU_BLOCK, LRU_BLOCK)
    z = jnp.zeros((N_GROUPS, LRU_BLOCK, LRU_BLOCK), w.dtype)
    top = jnp.concatenate([w[:, 0], z], axis=-1)
    bot = jnp.concatenate([z, w[:, 1]], axis=-1)
    return jnp.concatenate([top, bot], axis=-2)


def _mlp_kernel(x_ref, xp_ref, xn_ref, a_ref, ap_ref, an_ref, l_ref, lp_ref, ln_ref, mod_ref, ln2_ref,
                wout_ref, wup_ref, cw_ref, wdn_ref, o_ref, al_s, x1_s, n2_s, acc_s, u_s, *, tm, t):
    i = pl.program_id(1)
    n_ext = tm + 2 * HALO
    for src, lane0 in (((ap_ref, a_ref, an_ref), 0), ((lp_ref, l_ref, ln_ref), D_ATT)):
        prev_ref, main_ref, next_ref = src
        for p in range(N_GROUPS):
            lanes = slice(lane0 + p * LANES, lane0 + (p + 1) * LANES)
            al_s[0:HALO, lanes] = prev_ref[p]
            al_s[HALO:HALO + tm, lanes] = main_ref[p]
            al_s[HALO + tm:, lanes] = next_ref[p]
    x1_s[0:HALO, :] = xp_ref[...]
    x1_s[HALO:HALO + tm, :] = x_ref[...]
    x1_s[HALO + tm:, :] = xn_ref[...]

    rows_per_block = n_ext // PROLOGUE_BLOCKS
    assert rows_per_block * PROLOGUE_BLOCKS == n_ext and rows_per_block % HALO == 0
    for blk in range(PROLOGUE_BLOCKS):
        r0, r1 = blk * rows_per_block, (blk + 1) * rows_per_block
        mix = jnp.dot(al_s[r0:r1, :], wout_ref[...], preferred_element_type=F32)
        x1 = x1_s[r0:r1, :] + mod_ref[2:3, :] * mix
        x1_s[r0:r1, :] = x1
        ms = jnp.mean(x1 * x1, axis=-1, keepdims=True)
        n2 = x1 * lax.rsqrt(ms + NORM_EPS) * ln2_ref[...] * (1.0 + mod_ref[4:5, :]) + mod_ref[3:4, :]
        tok = i * tm - HALO + r0 + lax.broadcasted_iota(jnp.int32, (r1 - r0, 1), 0)
        n2_s[r0:r1, :] = jnp.where((tok >= 0) & (tok < t), n2, 0.0).astype(BF16)

    def up(c, slot):
        c0, c1 = FF_CHUNKS[c]
        n2 = n2_s[...]
        u_s[slot, 0, :, 0:c1 - c0] = jnp.dot(n2, wup_ref[:, c0:c1], preferred_element_type=F32)
        u_s[slot, 1, :, 0:c1 - c0] = jnp.dot(n2, wup_ref[:, D_FF + c0:D_FF + c1], preferred_element_type=F32)

    def conv(slot, half, c0, c1):
        w = c1 - c0
        cw = cw_ref[:, half * D_FF + c0:half * D_FF + c1]
        c = cw[1:2, :] * u_s[slot, half, HALO:HALO + tm, 0:w] + cw[3:4, :]
        c = c + cw[0:1, :] * u_s[slot, half, HALO - 1:HALO - 1 + tm, 0:w]
        return c + cw[2:3, :] * u_s[slot, half, HALO + 1:HALO + 1 + tm, 0:w]

    def down(c, slot):
        c0, c1 = FF_CHUNKS[c]
        ug = conv(slot, 0, c0, c1)
        uv = conv(slot, 1, c0, c1)
        act = (ug * (1.0 + _gelu_inner_tanh(ug)) * uv).astype(BF16)
        d = jnp.dot(act, wdn_ref[c0:c1, :], preferred_element_type=F32)
        acc_s[...] = d if c == 0 else acc_s[...] + d

    up(0, 0)
    for c in range(len(FF_CHUNKS)):
        if c + 1 < len(FF_CHUNKS):
            up(c + 1, (c + 1) % 2)
        down(c, c % 2)
    o_ref[...] = x1_s[HALO:HALO + tm, :] + mod_ref[5:6, :] * acc_s[...]


def _out_mlp(x, att, lru, mod, ln2_g, w_out_bf, w_up_bf, ffn_cw, w_dn_bf, tm):
    bsz, t, _ = x.shape
    per_halo = tm // HALO
    n_halo = t // HALO

    def main_map(b, i):
        return (b, i, 0)

    def prev_map(b, i):
        return (b, jnp.maximum(i * per_halo - 1, 0), 0)

    def next_map(b, i):
        return (b, jnp.minimum((i + 1) * per_halo, n_halo - 1), 0)

    def grouped(rows, imap):
        return pl.BlockSpec((None, N_GROUPS, rows, LANES), lambda b, i: (imap(b, i)[0], 0, imap(b, i)[1], 0))

    n_ext = tm + 2 * HALO
    return pl.pallas_call(
        functools.partial(_mlp_kernel, tm=tm, t=t),
        grid=(bsz, t // tm),
        in_specs=[
            pl.BlockSpec((None, tm, D_MODEL), main_map),
            pl.BlockSpec((None, HALO, D_MODEL), prev_map),
            pl.BlockSpec((None, HALO, D_MODEL), next_map),
            grouped(tm, main_map), grouped(HALO, prev_map), grouped(HALO, next_map),
            grouped(tm, main_map), grouped(HALO, prev_map), grouped(HALO, next_map),
            pl.BlockSpec((None, SUBLANES, D_MODEL), lambda b, i: (b, 0, 0)),
            _const_spec((1, D_MODEL)),
            _const_spec((D_MODEL, D_MODEL)),
            _const_spec((D_MODEL, 2 * D_FF)),
            _const_spec((SUBLANES, 2 * D_FF)),
            _const_spec((D_FF, D_MODEL)),
        ],
        out_specs=pl.BlockSpec((None, tm, D_MODEL), main_map),
        out_shape=jax.ShapeDtypeStruct(x.shape, F32),
        scratch_shapes=[pltpu.VMEM((n_ext, D_MODEL), BF16),
                        pltpu.VMEM((n_ext, D_MODEL), F32),
                        pltpu.VMEM((n_ext, D_MODEL), BF16),
                        pltpu.VMEM((tm, D_MODEL), F32),
                        pltpu.VMEM((2, 2, n_ext, FF_CHUNK_MAX), F32)],
        compiler_params=pltpu.CompilerParams(
            dimension_semantics=("parallel", "parallel"), vmem_limit_bytes=VMEM_LIMIT),
        name="out_mlp",
    )(x, x, x, att, att, att, lru, lru, lru, mod, ln2_g, w_out_bf, w_up_bf, ffn_cw, w_dn_bf)


def _pad_rows(a, rows):
    return jnp.concatenate([a, jnp.zeros((rows - a.shape[0],) + a.shape[1:], a.dtype)], axis=0)


def _prepare_params(ln1_g, ln2_g, w_in, q_norm_g, k_norm_g, rpb, lru_conv_w, lru_conv_b,
                    w_r_f, b_r_f, w_i_f, b_i_f, lam_f, w_r_b, b_r_b, w_i_b, b_i_b, lam_b,
                    w_out, w_up, ffn_conv_w, ffn_conv_b, w_down):
    head = jnp.arange(MXU_WIDTH) // HEAD_DIM
    gmat = (head[:, None] == head[None, :]).astype(BF16)
    scale = HEAD_DIM ** -0.5 * LOG2_E
    qkg = _pad_rows(jnp.stack([jnp.tile(q_norm_g.astype(F32), ATT_HEADS) * scale,
                               jnp.tile(k_norm_g.astype(F32), ATT_HEADS)]), SUBLANES)
    lru_cw = _pad_rows(jnp.concatenate([lru_conv_w, lru_conv_b[None]], axis=0), SUBLANES)
    lru_cw = jnp.transpose(lru_cw.reshape(SUBLANES, N_GROUPS, LANES), (1, 0, 2))
    wf = (0.5 * jnp.concatenate([_block_diag_pairs(w_r_f), _block_diag_pairs(w_i_f)], axis=-1)).astype(BF16)
    wb = (0.5 * jnp.concatenate([_block_diag_pairs(w_r_b), _block_diag_pairs(w_i_b)], axis=-1)).astype(BF16)
    par = _pad_rows(jnp.stack([0.5 * b_r_f, 0.5 * b_i_f, lam_f, 0.5 * b_r_b, 0.5 * b_i_b, lam_b]), SUBLANES)
    par = jnp.transpose(par.reshape(SUBLANES, N_GROUPS, LANES), (1, 0, 2))
    ffn_cw = _pad_rows(jnp.concatenate([ffn_conv_w, ffn_conv_b[None]], axis=0), SUBLANES)
    ffn_cw = ffn_cw * jnp.where(jnp.arange(2 * D_FF) < D_FF, 1.0, 0.5)[None, :]
    return dict(
        ln1_g=ln1_g.reshape(1, D_MODEL), ln2_g=ln2_g.reshape(1, D_MODEL),
        w_in=w_in.astype(BF16), gmat=gmat, qkg=qkg, bias=_attention_bias(rpb),
        lru_cw=lru_cw, wf=wf, wb=wb, par=par,
        w_out=w_out.astype(BF16), w_up=w_up.astype(BF16), ffn_cw=ffn_cw, w_down=w_down.astype(BF16))


def _encoder_layer(x, mod, p, tm, tc):
    q, k, v, xl, gl = _in_projection(x, mod, p["ln1_g"], p["w_in"], p["gmat"], p["qkg"], tm)
    att = _attention(q, k, v, p["bias"])
    lru = _rg_lru(xl, gl, p["lru_cw"], p["wf"], p["wb"], p["par"], tc)
    return _out_mlp(x, att, lru, mod, p["ln2_g"], p["w_out"], p["w_up"], p["ffn_cw"], p["w_down"], tm)


def _forward(x_prompt, x_sample, c_prompt, c_sample, w_ada, b_ada, layer_params, tm=512, tc=512):
    nb_p, nb_s = c_prompt.shape[0], c_sample.shape[0]
    depth = w_ada.shape[0]
    for layer in range(depth):
        c_all = jnp.concatenate([c_prompt, c_sample], axis=0)
        c_all = _pad_rows(c_all, -(-c_all.shape[0] // SUBLANES) * SUBLANES)
        mod = _modulation(c_all, w_ada[layer], b_ada[layer])
        mod = mod.reshape(mod.shape[0], N_MOD, D_MODEL)
        mod = jnp.concatenate([mod, jnp.zeros((mod.shape[0], SUBLANES - N_MOD, D_MODEL), F32)], axis=1)
        p = _prepare_params(*[w[layer] for w in layer_params])
        x_prompt = _encoder_layer(x_prompt, mod[:nb_p], p, tm, tc)
        x_sample = _encoder_layer(x_sample, mod[nb_p:nb_p + nb_s], p, tm, tc)
    return x_prompt, x_sample


def kernel(x_prompt, x_sample, c_prompt, c_sample, ln1_g, ln2_g, w_ada, b_ada, w_in, q_norm_g, k_norm_g, rpb,
           lru_conv_w, lru_conv_b, w_r_f, b_r_f, w_i_f, b_i_f, lam_f, w_r_b, b_r_b, w_i_b, b_i_b, lam_b,
           w_out, w_up, ffn_conv_w, ffn_conv_b, w_down):
    layer_params = (ln1_g, ln2_g, w_in, q_norm_g, k_norm_g, rpb, lru_conv_w, lru_conv_b,
                    w_r_f, b_r_f, w_i_f, b_i_f, lam_f, w_r_b, b_r_b, w_i_b, b_i_b, lam_b,
                    w_out, w_up, ffn_conv_w, ffn_conv_b, w_down)
    return _forward(x_prompt, x_sample, c_prompt, c_sample, w_ada, b_ada, layer_params)
```

```python
import functools
import math

import numpy as np

import jax
import jax.numpy as jnp
from jax import lax
from jax.experimental import pallas as pl
from jax.experimental.pallas import tpu as pltpu

D_MODEL = 1024
GRID_W = 64
ATT_HEADS = 8
HEAD_DIM = 64
D_ATT = ATT_HEADS * HEAD_DIM
D_LRU = 512
LRU_HEADS = 8
LRU_BLOCK = D_LRU // LRU_HEADS
NA_ROWS = 8
NA_COLS = 16
LRU_CONV = 4
LRU_C = 8.0
FFN_CONV = 3
D_FF = 2816
D_IN = 3 * D_ATT + 2 * D_LRU
N_MOD = 6
NORM_EPS = 1e-6

LANES = 128
SUBLANES = 8
MXU_WIDTH = 256
N_GROUPS = D_ATT // LANES
WIN_KEYS = NA_ROWS * GRID_W
FF_CHUNKS = ((0, 7 * MXU_WIDTH), (7 * MXU_WIDTH, D_FF))
FF_CHUNK_MAX = max(c1 - c0 for c0, c1 in FF_CHUNKS)
HALO = 16
PROLOGUE_BLOCKS = 2
ROW_UNROLL = 32
LRU_STEPS = 256
LRU_SEG_PAD = 8
MASK_VALUE = -1e30
LOG2_E = math.log2(math.e)
TINY = 1e-30
VMEM_LIMIT = 56 * 1024 * 1024

BF16 = jnp.bfloat16
F32 = jnp.float32


def _gelu_tanh(x):
    return 0.5 * x * (1.0 + jnp.tanh(math.sqrt(2.0 / math.pi) * (x + 0.044715 * (x * x * x))))


def _gelu_inner_tanh(x):
    c = math.sqrt(2.0 / math.pi)
    return jnp.tanh(x * (c + (0.044715 * c) * (x * x)))


def _sigmoid(x):
    return 0.5 * jnp.tanh(0.5 * x) + 0.5


def _const_spec(shape):
    zeros = (0,) * len(shape)
    return pl.BlockSpec(shape, lambda *_: zeros, pipeline_mode=pl.Buffered(1))


def _mod_kernel(c_ref, w_ref, b_ref, o_ref):
    c = c_ref[...]
    s = (c * _sigmoid(c)).astype(BF16)
    o_ref[...] = jnp.dot(s, w_ref[...].astype(BF16), preferred_element_type=F32) + b_ref[...]


def _modulation(c_all, w_ada, b_ada):
    bp = c_all.shape[0]
    return pl.pallas_call(
        _mod_kernel,
        grid=(N_MOD,),
        in_specs=[
            pl.BlockSpec((bp, D_MODEL), lambda j: (0, 0)),
            pl.BlockSpec((D_MODEL, D_MODEL), lambda j: (0, j)),
            pl.BlockSpec((1, D_MODEL), lambda j: (0, j)),
        ],
        out_specs=pl.BlockSpec((bp, D_MODEL), lambda j: (0, j)),
        out_shape=jax.ShapeDtypeStruct((bp, N_MOD * D_MODEL), F32),
        compiler_params=pltpu.CompilerParams(dimension_semantics=("arbitrary",), vmem_limit_bytes=VMEM_LIMIT),
        name="modulation",
    )(c_all, w_ada, b_ada.reshape(1, -1))


def _inproj_kernel(x_ref, mod_ref, ln_ref, w_ref, gmat_ref, qkg_ref,
                   q_ref, k_ref, v_ref, xl_ref, gl_ref):
    x = x_ref[...]
    ms = jnp.mean(x * x, axis=-1, keepdims=True)
    y = x * lax.rsqrt(ms + NORM_EPS) * ln_ref[...]
    n = (y * (1.0 + mod_ref[1:2, :]) + mod_ref[0:1, :]).astype(BF16)

    def head_norm(z, gain):
        zz = (z * z).astype(BF16)
        width = gmat_ref.shape[0]
        ss = jnp.concatenate(
            [jnp.dot(zz[:, c:c + width], gmat_ref[...], preferred_element_type=F32) for c in range(0, D_ATT, width)],
            axis=1)
        return z * lax.rsqrt(ss * (1.0 / HEAD_DIM) + NORM_EPS) * gain

    outs = (q_ref, k_ref, v_ref, xl_ref, gl_ref)
    for idx, o_ref in enumerate(outs):
        z = jnp.dot(n, w_ref[:, idx * D_ATT:(idx + 1) * D_ATT], preferred_element_type=F32)
        if idx < 2:
            z = head_norm(z, qkg_ref[idx:idx + 1, :])
        for p in range(N_GROUPS):
            o_ref[p] = z[:, p * LANES:(p + 1) * LANES].astype(BF16)


def _in_projection(x, mod, ln1_g, w_in_bf, gmat, qkg, tm):
    bsz, t, _ = x.shape
    grouped = jax.ShapeDtypeStruct((bsz, N_GROUPS, t, LANES), BF16)
    out_spec = pl.BlockSpec((None, N_GROUPS, tm, LANES), lambda b, i: (b, 0, i, 0))
    return pl.pallas_call(
        _inproj_kernel,
        grid=(bsz, t // tm),
        in_specs=[
            pl.BlockSpec((None, tm, D_MODEL), lambda b, i: (b, i, 0)),
            pl.BlockSpec((None, SUBLANES, D_MODEL), lambda b, i: (b, 0, 0)),
            _const_spec((1, D_MODEL)),
            _const_spec((D_MODEL, D_IN)),
            _const_spec((MXU_WIDTH, MXU_WIDTH)),
            _const_spec((SUBLANES, D_ATT)),
        ],
        out_specs=[out_spec] * 5,
        out_shape=[grouped] * 5,
        compiler_params=pltpu.CompilerParams(
            dimension_semantics=("parallel", "parallel"), vmem_limit_bytes=VMEM_LIMIT),
        name="in_projection",
    )(x, mod, ln1_g, w_in_bf, gmat, qkg)


def _attn_kernel(q_ref, k_ref, v_ref, bias_ref, o_ref, *, rows):
    lane = lax.broadcasted_iota(jnp.int32, (GRID_W, LANES), 1)
    low = lane < HEAD_DIM

    def scores(r):
        rs = jnp.clip(r - NA_ROWS // 2, 0, rows - NA_ROWS)
        q0 = pl.multiple_of(r * GRID_W, GRID_W)
        k0 = pl.multiple_of(rs * GRID_W, GRID_W)
        q2 = q_ref[pl.ds(q0, GRID_W), :].astype(F32)
        qq = jnp.concatenate([jnp.where(low, q2, 0.0), jnp.where(low, 0.0, q2)], axis=0).astype(BF16)
        k2 = k_ref[pl.ds(k0, WIN_KEYS), :]
        s = lax.dot_general(qq, k2, (((1,), (1,)), ((), ())), preferred_element_type=F32)
        return s + bias_ref[r - rs], q0, k0

    def softmax(s):
        m = jnp.max(s, axis=-1, keepdims=True)
        return jnp.exp2(s - m).astype(BF16)

    ones = jnp.ones((WIN_KEYS, LANES), BF16)

    def weighted(e, q0, k0):
        v2 = jnp.concatenate([v_ref[pl.ds(k0, WIN_KEYS), :], ones], axis=1)
        pv = jnp.dot(e, v2, preferred_element_type=F32)
        pv = pv[:, :LANES] * (1.0 / pv[:, LANES:])
        out = jnp.where(low, pv[:GRID_W], pv[GRID_W:])
        o_ref[pl.ds(q0, GRID_W), :] = out.astype(BF16)

    def group_step(g, carry):
        staged = [scores(g * ROW_UNROLL + i) for i in range(ROW_UNROLL)]
        probs = [softmax(s) for s, _, _ in staged]
        for e, (_, q0, k0) in zip(probs, staged):
            weighted(e, q0, k0)
        return carry

    lax.fori_loop(0, rows // ROW_UNROLL, group_step, 0)


def _attention(q, k, v, bias):
    bsz, _, t, _ = q.shape
    rows = t // GRID_W
    assert rows % ROW_UNROLL == 0 and rows >= NA_ROWS
    seq_spec = pl.BlockSpec((None, None, t, LANES), lambda p, b: (b, p, 0, 0))
    return pl.pallas_call(
        functools.partial(_attn_kernel, rows=rows),
        grid=(N_GROUPS, bsz),
        in_specs=[seq_spec, seq_spec, seq_spec,
                  pl.BlockSpec((None, NA_ROWS, 2 * GRID_W, WIN_KEYS), lambda p, b: (p, 0, 0, 0))],
        out_specs=seq_spec,
        out_shape=jax.ShapeDtypeStruct(q.shape, BF16),
        compiler_params=pltpu.CompilerParams(
            dimension_semantics=("parallel", "parallel"), vmem_limit_bytes=VMEM_LIMIT),
        name="nbr_attention",
    )(q, k, v, bias)


def _attention_bias(rpb):
    cols = np.arange(GRID_W)
    col_start = np.clip(cols - NA_COLS // 2, 0, GRID_W - NA_COLS)
    in_win = (cols[None, :] >= col_start[:, None]) & (cols[None, :] < col_start[:, None] + NA_COLS)
    dc = cols[None, :] - cols[:, None] + (NA_COLS - 1)
    onehot = (np.arange(2 * NA_COLS - 1)[:, None, None] == dc[None]) & in_win[None]
    tab = jnp.einsum("hrd,dcx->hrcx", rpb.astype(F32), jnp.asarray(onehot, F32),
                     precision=lax.Precision.HIGHEST)
    tab = jnp.where(jnp.asarray(in_win)[None, None], tab * LOG2_E, MASK_VALUE)
    b = jnp.stack([tab[:, NA_ROWS - 1 - o:2 * NA_ROWS - 1 - o] for o in range(NA_ROWS)], axis=1)
    b = jnp.transpose(b, (0, 1, 3, 2, 4)).reshape(ATT_HEADS, NA_ROWS, GRID_W, WIN_KEYS)
    b = b.reshape(N_GROUPS, 2, NA_ROWS, GRID_W, WIN_KEYS)
    return jnp.transpose(b, (0, 2, 1, 3, 4)).reshape(N_GROUPS, NA_ROWS, 2 * GRID_W, WIN_KEYS)


def _lru_kernel(xl_ref, gl_ref, cw_ref, wf_ref, wb_ref, par_ref, o_ref,
                xp_s, xc_s, hf_s, qf_s, hb_s, qb_s, *, t, tc):
    seg = t // SUBLANES
    pitch = seg + LRU_SEG_PAD
    n_conv = t // tc
    conv_per_seg = seg // tc
    n_steps = seg // LRU_STEPS
    zeros8 = jnp.zeros((SUBLANES, LANES), F32)
    ones8 = jnp.ones((SUBLANES, LANES), F32)
    xp_s[0:SUBLANES, :] = zeros8
    xp_s[SUBLANES + t:, :] = zeros8
    xp_s[SUBLANES:SUBLANES + t, :] = xl_ref[...].astype(F32)

    def softplus(z):
        return jnp.maximum(z, 0.0) + jnp.log(1.0 + jnp.exp(-jnp.abs(z)))

    half_decay = [(-0.5 * LRU_C) * softplus(-par_ref[row:row + 1, :]) for row in (2, 5)]

    def gates(xc, w_ref, direction):
        row0 = 3 * direction
        g = jnp.dot(xc.astype(BF16), w_ref[...], preferred_element_type=F32)
        tr = jnp.tanh(g[:, :LANES] + par_ref[row0:row0 + 1, :])
        ti = jnp.tanh(g[:, LANES:] + par_ref[row0 + 1:row0 + 2, :])
        log_a = half_decay[direction] * tr + half_decay[direction]
        a = jnp.exp(log_a)
        beta_sq = (a * a + 1.0) * (-jnp.tanh(log_a))
        beta = beta_sq * lax.rsqrt(jnp.maximum(beta_sq, TINY))
        u = beta * ((0.5 * ti + 0.5) * xc)
        return a, u

    for c in range(n_conv):
        lo = SUBLANES + c * tc
        acc = cw_ref[1:2, :] * xp_s[lo:lo + tc, :] + cw_ref[4:5, :]
        for tap in (0, 2, 3):
            acc = acc + cw_ref[tap:tap + 1, :] * xp_s[lo + tap - 1:lo + tap - 1 + tc, :]
        r0 = (c // conv_per_seg) * pitch + (c % conv_per_seg) * tc
        xc_s[r0:r0 + tc, :] = acc

    def tile(ref, k):
        return ref.at[pl.ds(k, SUBLANES, stride=pitch), :]

    def gather(k0):
        return jnp.concatenate([tile(xc_s, k0 + kk)[...] for kk in range(LRU_STEPS)], axis=0)

    def scan_step(c, carry):
        hf, qf, hb, qb = carry
        kf = c * LRU_STEPS
        kb = (n_steps - 1 - c) * LRU_STEPS
        af, uf = gates(gather(kf), wf_ref, 0)
        ab, ub = gates(gather(kb), wb_ref, 1)
        for kk in range(LRU_STEPS):
            rows = slice(kk * SUBLANES, (kk + 1) * SUBLANES)
            hf = af[rows] * hf + uf[rows]
            qf = af[rows] * qf
            tile(hf_s, kf + kk)[...] = hf
            tile(qf_s, kf + kk)[...] = qf
            kr = LRU_STEPS - 1 - kk
            rows = slice(kr * SUBLANES, (kr + 1) * SUBLANES)
            hb = ab[rows] * hb + ub[rows]
            qb = ab[rows] * qb
            tile(hb_s, kb + kr)[...] = hb
            tile(qb_s, kb + kr)[...] = qb
        return hf, qf, hb, qb

    hf, qf, hb, qb = lax.fori_loop(0, n_steps, scan_step, (zeros8, ones8, zeros8, ones8))

    cf = [jnp.zeros((1, LANES), F32)]
    for s in range(SUBLANES - 1):
        cf.append(qf[s:s + 1] * cf[s] + hf[s:s + 1])
    cb = [jnp.zeros((1, LANES), F32)]
    for s in range(SUBLANES - 1, 0, -1):
        cb.insert(0, qb[s:s + 1] * cb[0] + hb[s:s + 1])

    for c in range(n_conv):
        s = c // conv_per_seg
        r0 = s * pitch + (c % conv_per_seg) * tc
        h = (hf_s[r0:r0 + tc, :] + qf_s[r0:r0 + tc, :] * cf[s]) + (hb_s[r0:r0 + tc, :] + qb_s[r0:r0 + tc, :] * cb[s])
        gate = _gelu_tanh(gl_ref[c * tc:(c + 1) * tc, :].astype(F32))
        o_ref[c * tc:(c + 1) * tc, :] = (h * gate).astype(BF16)


def _rg_lru(xl, gl, cw, wf, wb, par, tc):
    bsz, _, t, _ = xl.shape
    seg = t // SUBLANES
    tc = min(tc, seg)
    assert seg % tc == 0 and seg % LRU_STEPS == 0
    pitched = SUBLANES * (seg + LRU_SEG_PAD)
    seq_spec = pl.BlockSpec((None, None, t, LANES), lambda b, g: (b, g, 0, 0))

    def group_spec(shape):
        return pl.BlockSpec((None,) + shape, lambda b, g: (g,) + (0,) * len(shape))

    return pl.pallas_call(
        functools.partial(_lru_kernel, t=t, tc=tc),
        grid=(bsz, N_GROUPS),
        in_specs=[seq_spec, seq_spec, group_spec((SUBLANES, LANES)), group_spec((LANES, 2 * LANES)),
                  group_spec((LANES, 2 * LANES)), group_spec((SUBLANES, LANES))],
        out_specs=seq_spec,
        out_shape=jax.ShapeDtypeStruct(xl.shape, BF16),
        scratch_shapes=[pltpu.VMEM((t + 2 * SUBLANES, LANES), F32)] + [pltpu.VMEM((pitched, LANES), F32)] * 5,
        compiler_params=pltpu.CompilerParams(
            dimension_semantics=("parallel", "parallel"), vmem_limit_bytes=VMEM_LIMIT),
        name="rg_lru",
    )(xl, gl, cw, wf, wb, par)


def _block_diag_pairs(w):
    w = w.reshape(N_GROUPS, 2, LRU_BLOCK, LRU_BLOCK)
    z = jnp.zeros((N_GROUPS, LRU_BLOCK, LRU_BLOCK), w.dtype)
    top = jnp.concatenate([w[:, 0], z], axis=-1)
    bot = jnp.concatenate([z, w[:, 1]], axis=-1)
    return jnp.concatenate([top, bot], axis=-2)


def _mlp_kernel(x_ref, xp_ref, xn_ref, a_ref, ap_ref, an_ref, l_ref, lp_ref, ln_ref, mod_ref, ln2_ref,
                wout_ref, wup_ref, cw_ref, wdn_ref, o_ref, al_s, x1_s, n2_s, acc_s, u_s, *, tm, t):
    i = pl.program_id(1)
    n_ext = tm + 2 * HALO
    for src, lane0 in (((ap_ref, a_ref, an_ref), 0), ((lp_ref, l_ref, ln_ref), D_ATT)):
        prev_ref, main_ref, next_ref = src
        for p in range(N_GROUPS):
            lanes = slice(lane0 + p * LANES, lane0 + (p + 1) * LANES)
            al_s[0:HALO, lanes] = prev_ref[p]
            al_s[HALO:HALO + tm, lanes] = main_ref[p]
            al_s[HALO + tm:, lanes] = next_ref[p]
    x1_s[0:HALO, :] = xp_ref[...]
    x1_s[HALO:HALO + tm, :] = x_ref[...]
    x1_s[HALO + tm:, :] = xn_ref[...]

    rows_per_block = n_ext // PROLOGUE_BLOCKS
    assert rows_per_block * PROLOGUE_BLOCKS == n_ext and rows_per_block % HALO == 0
    for blk in range(PROLOGUE_BLOCKS):
        r0, r1 = blk * rows_per_block, (blk + 1) * rows_per_block
        mix = jnp.dot(al_s[r0:r1, :], wout_ref[...], preferred_element_type=F32)
        x1 = x1_s[r0:r1, :] + mod_ref[2:3, :] * mix
        x1_s[r0:r1, :] = x1
        ms = jnp.mean(x1 * x1, axis=-1, keepdims=True)
        n2 = x1 * lax.rsqrt(ms + NORM_EPS) * ln2_ref[...] * (1.0 + mod_ref[4:5, :]) + mod_ref[3:4, :]
        tok = i * tm - HALO + r0 + lax.broadcasted_iota(jnp.int32, (r1 - r0, 1), 0)
        n2_s[r0:r1, :] = jnp.where((tok >= 0) & (tok < t), n2, 0.0).astype(BF16)

    def up(c, slot):
        c0, c1 = FF_CHUNKS[c]
        n2 = n2_s[...]
        u_s[slot, 0, :, 0:c1 - c0] = jnp.dot(n2, wup_ref[:, c0:c1], preferred_element_type=F32)
        u_s[slot, 1, :, 0:c1 - c0] = jnp.dot(n2, wup_ref[:, D_FF + c0:D_FF + c1], preferred_element_type=F32)

    def conv(slot, half, c0, c1):
        w = c1 - c0
        cw = cw_ref[:, half * D_FF + c0:half * D_FF + c1]
        c = cw[1:2, :] * u_s[slot, half, HALO:HALO + tm, 0:w] + cw[3:4, :]
        c = c + cw[0:1, :] * u_s[slot, half, HALO - 1:HALO - 1 + tm, 0:w]
        return c + cw[2:3, :] * u_s[slot, half, HALO + 1:HALO + 1 + tm, 0:w]

    def down(c, slot):
        c0, c1 = FF_CHUNKS[c]
        ug = conv(slot, 0, c0, c1)
        uv = conv(slot, 1, c0, c1)
        act = (ug * (1.0 + _gelu_inner_tanh(ug)) * uv).astype(BF16)
        d = jnp.dot(act, wdn_ref[c0:c1, :], preferred_element_type=F32)
        acc_s[...] = d if c == 0 else acc_s[...] + d

    up(0, 0)
    for c in range(len(FF_CHUNKS)):
        if c + 1 < len(FF_CHUNKS):
            up(c + 1, (c + 1) % 2)
        down(c, c % 2)
    o_ref[...] = x1_s[HALO:HALO + tm, :] + mod_ref[5:6, :] * acc_s[...]


def _out_mlp(x, att, lru, mod, ln2_g, w_out_bf, w_up_bf, ffn_cw, w_dn_bf, tm):
    bsz, t, _ = x.shape
    per_halo = tm // HALO
    n_halo = t // HALO

    def main_map(b, i):
        return (b, i, 0)

    def prev_map(b, i):
        return (b, jnp.maximum(i * per_halo - 1, 0), 0)

    def next_map(b, i):
        return (b, jnp.minimum((i + 1) * per_halo, n_halo - 1), 0)

    def grouped(rows, imap):
        return pl.BlockSpec((None, N_GROUPS, rows, LANES), lambda b, i: (imap(b, i)[0], 0, imap(b, i)[1], 0))

    n_ext = tm + 2 * HALO
    return pl.pallas_call(
        functools.partial(_mlp_kernel, tm=tm, t=t),
        grid=(bsz, t // tm),
        in_specs=[
            pl.BlockSpec((None, tm, D_MODEL), main_map),
            pl.BlockSpec((None, HALO, D_MODEL), prev_map),
            pl.BlockSpec((None, HALO, D_MODEL), next_map),
            grouped(tm, main_map), grouped(HALO, prev_map), grouped(HALO, next_map),
            grouped(tm, main_map), grouped(HALO, prev_map), grouped(HALO, next_map),
            pl.BlockSpec((None, SUBLANES, D_MODEL), lambda b, i: (b, 0, 0)),
            _const_spec((1, D_MODEL)),
            _const_spec((D_MODEL, D_MODEL)),
            _const_spec((D_MODEL, 2 * D_FF)),
            _const_spec((SUBLANES, 2 * D_FF)),
            _const_spec((D_FF, D_MODEL)),
        ],
        out_specs=pl.BlockSpec((None, tm, D_MODEL), main_map),
        out_shape=jax.ShapeDtypeStruct(x.shape, F32),
        scratch_shapes=[pltpu.VMEM((n_ext, D_MODEL), BF16),
                        pltpu.VMEM((n_ext, D_MODEL), F32),
                        pltpu.VMEM((n_ext, D_MODEL), BF16),
                        pltpu.VMEM((tm, D_MODEL), F32),
                        pltpu.VMEM((2, 2, n_ext, FF_CHUNK_MAX), F32)],
        compiler_params=pltpu.CompilerParams(
            dimension_semantics=("parallel", "parallel"), vmem_limit_bytes=VMEM_LIMIT),
        name="out_mlp",
    )(x, x, x, att, att, att, lru, lru, lru, mod, ln2_g, w_out_bf, w_up_bf, ffn_cw, w_dn_bf)


def _pad_rows(a, rows):
    return jnp.concatenate([a, jnp.zeros((rows - a.shape[0],) + a.shape[1:], a.dtype)], axis=0)


def _prepare_params(ln1_g, ln2_g, w_in, q_norm_g, k_norm_g, rpb, lru_conv_w, lru_conv_b,
                    w_r_f, b_r_f, w_i_f, b_i_f, lam_f, w_r_b, b_r_b, w_i_b, b_i_b, lam_b,
                    w_out, w_up, ffn_conv_w, ffn_conv_b, w_down):
    head = jnp.arange(MXU_WIDTH) // HEAD_DIM
    gmat = (head[:, None] == head[None, :]).astype(BF16)
    scale = HEAD_DIM ** -0.5 * LOG2_E
    qkg = _pad_rows(jnp.stack([jnp.tile(q_norm_g.astype(F32), ATT_HEADS) * scale,
                               jnp.tile(k_norm_g.astype(F32), ATT_HEADS)]), SUBLANES)
    lru_cw = _pad_rows(jnp.concatenate([lru_conv_w, lru_conv_b[None]], axis=0), SUBLANES)
    lru_cw = jnp.transpose(lru_cw.reshape(SUBLANES, N_GROUPS, LANES), (1, 0, 2))
    wf = (0.5 * jnp.concatenate([_block_diag_pairs(w_r_f), _block_diag_pairs(w_i_f)], axis=-1)).astype(BF16)
    wb = (0.5 * jnp.concatenate([_block_diag_pairs(w_r_b), _block_diag_pairs(w_i_b)], axis=-1)).astype(BF16)
    par = _pad_rows(jnp.stack([0.5 * b_r_f, 0.5 * b_i_f, lam_f, 0.5 * b_r_b, 0.5 * b_i_b, lam_b]), SUBLANES)
    par = jnp.transpose(par.reshape(SUBLANES, N_GROUPS, LANES), (1, 0, 2))
    ffn_cw = _pad_rows(jnp.concatenate([ffn_conv_w, ffn_conv_b[None]], axis=0), SUBLANES)
    ffn_cw = ffn_cw * jnp.where(jnp.arange(2 * D_FF) < D_FF, 1.0, 0.5)[None, :]
    return dict(
        ln1_g=ln1_g.reshape(1, D_MODEL), ln2_g=ln2_g.reshape(1, D_MODEL),
        w_in=w_in.astype(BF16), gmat=gmat, qkg=qkg, bias=_attention_bias(rpb),
        lru_cw=lru_cw, wf=wf, wb=wb, par=par,
        w_out=w_out.astype(BF16), w_up=w_up.astype(BF16), ffn_cw=ffn_cw, w_down=w_down.astype(BF16))


def _encoder_layer(x, mod, p, tm, tc):
    q, k, v, xl, gl = _in_projection(x, mod, p["ln1_g"], p["w_in"], p["gmat"], p["qkg"], tm)
    att = _attention(q, k, v, p["bias"])
    lru = _rg_lru(xl, gl, p["lru_cw"], p["wf"], p["wb"], p["par"], tc)
    return _out_mlp(x, att, lru, mod, p["ln2_g"], p["w_out"], p["w_up"], p["ffn_cw"], p["w_down"], tm)


def _forward(x_prompt, x_sample, c_prompt, c_sample, w_ada, b_ada, layer_params, tm=512, tc=512):
    nb_p, nb_s = c_prompt.shape[0], c_sample.shape[0]
    depth = w_ada.shape[0]
    for layer in range(depth):
        c_all = jnp.concatenate([c_prompt, c_sample], axis=0)
        c_all = _pad_rows(c_all, -(-c_all.shape[0] // SUBLANES) * SUBLANES)
        mod = _modulation(c_all, w_ada[layer], b_ada[layer])
        mod = mod.reshape(mod.shape[0], N_MOD, D_MODEL)
        mod = jnp.concatenate([mod, jnp.zeros((mod.shape[0], SUBLANES - N_MOD, D_MODEL), F32)], axis=1)
        p = _prepare_params(*[w[layer] for w in layer_params])
        x_prompt = _encoder_layer(x_prompt, mod[:nb_p], p, tm, tc)
        x_sample = _encoder_layer(x_sample, mod[nb_p:nb_p + nb_s], p, tm, tc)
    return x_prompt, x_sample


def kernel(x_prompt, x_sample, c_prompt, c_sample, ln1_g, ln2_g, w_ada, b_ada, w_in, q_norm_g, k_norm_g, rpb,
           lru_conv_w, lru_conv_b, w_r_f, b_r_f, w_i_f, b_i_f, lam_f, w_r_b, b_r_b, w_i_b, b_i_b, lam_b,
           w_out, w_up, ffn_conv_w, ffn_conv_b, w_down):
    layer_params = (ln1_g, ln2_g, w_in, q_norm_g, k_norm_g, rpb, lru_conv_w, lru_conv_b,
                    w_r_f, b_r_f, w_i_f, b_i_f, lam_f, w_r_b, b_r_b, w_i_b, b_i_b, lam_b,
                    w_out, w_up, ffn_conv_w, ffn_conv_b, w_down)
    return _forward(x_prompt, x_sample, c_prompt, c_sample, w_ada, b_ada, layer_params)
```

```python
import functools
import math

import numpy as np

import jax
import jax.numpy as jnp
from jax import lax
from jax.experimental import pallas as pl
from jax.experimental.pallas import tpu as pltpu

D_MODEL = 1024
GRID_W = 64
ATT_HEADS = 8
HEAD_DIM = 64
D_ATT = ATT_HEADS * HEAD_DIM
D_LRU = 512
LRU_HEADS = 8
LRU_BLOCK = D_LRU // LRU_HEADS
NA_ROWS = 8
NA_COLS = 16
LRU_CONV = 4
LRU_C = 8.0
FFN_CONV = 3
D_FF = 2816
D_IN = 3 * D_ATT + 2 * D_LRU
N_MOD = 6
NORM_EPS = 1e-6

LANES = 128
SUBLANES = 8
MXU_WIDTH = 256
N_GROUPS = D_ATT // LANES
WIN_KEYS = NA_ROWS * GRID_W
FF_CHUNKS = ((0, 7 * MXU_WIDTH), (7 * MXU_WIDTH, D_FF))
FF_CHUNK_MAX = max(c1 - c0 for c0, c1 in FF_CHUNKS)
HALO = 16
PROLOGUE_BLOCKS = 2
ROW_UNROLL = 64
LRU_STEPS = 256
LRU_SEG_PAD = 8
MASK_VALUE = -1e30
LOG2_E = math.log2(math.e)
TINY = 1e-30
VMEM_LIMIT = 56 * 1024 * 1024

BF16 = jnp.bfloat16
F32 = jnp.float32


def _gelu_tanh(x):
    return 0.5 * x * (1.0 + jnp.tanh(math.sqrt(2.0 / math.pi) * (x + 0.044715 * (x * x * x))))


def _gelu_inner_tanh(x):
    c = math.sqrt(2.0 / math.pi)
    return jnp.tanh(x * (c + (0.044715 * c) * (x * x)))


def _sigmoid(x):
    return 0.5 * jnp.tanh(0.5 * x) + 0.5


def _const_spec(shape):
    zeros = (0,) * len(shape)
    return pl.BlockSpec(shape, lambda *_: zeros, pipeline_mode=pl.Buffered(1))


def _mod_kernel(c_ref, w_ref, b_ref, o_ref):
    c = c_ref[...]
    s = (c * _sigmoid(c)).astype(BF16)
    o_ref[...] = jnp.dot(s, w_ref[...].astype(BF16), preferred_element_type=F32) + b_ref[...]


def _modulation(c_all, w_ada, b_ada):
    bp = c_all.shape[0]
    return pl.pallas_call(
        _mod_kernel,
        grid=(N_MOD,),
        in_specs=[
            pl.BlockSpec((bp, D_MODEL), lambda j: (0, 0)),
            pl.BlockSpec((D_MODEL, D_MODEL), lambda j: (0, j)),
            pl.BlockSpec((1, D_MODEL), lambda j: (0, j)),
        ],
        out_specs=pl.BlockSpec((bp, D_MODEL), lambda j: (0, j)),
        out_shape=jax.ShapeDtypeStruct((bp, N_MOD * D_MODEL), F32),
        compiler_params=pltpu.CompilerParams(dimension_semantics=("arbitrary",), vmem_limit_bytes=VMEM_LIMIT),
        name="modulation",
    )(c_all, w_ada, b_ada.reshape(1, -1))


def _inproj_kernel(x_ref, mod_ref, ln_ref, w_ref, gmat_ref, qkg_ref,
                   q_ref, k_ref, v_ref, xl_ref, gl_ref):
    x = x_ref[...]
    ms = jnp.mean(x * x, axis=-1, keepdims=True)
    y = x * lax.rsqrt(ms + NORM_EPS) * ln_ref[...]
    n = (y * (1.0 + mod_ref[1:2, :]) + mod_ref[0:1, :]).astype(BF16)

    def head_norm(z, gain):
        zz = (z * z).astype(BF16)
        width = gmat_ref.shape[0]
        ss = jnp.concatenate(
            [jnp.dot(zz[:, c:c + width], gmat_ref[...], preferred_element_type=F32) for c in range(0, D_ATT, width)],
            axis=1)
        return z * lax.rsqrt(ss * (1.0 / HEAD_DIM) + NORM_EPS) * gain

    outs = (q_ref, k_ref, v_ref, xl_ref, gl_ref)
    for idx, o_ref in enumerate(outs):
        z = jnp.dot(n, w_ref[:, idx * D_ATT:(idx + 1) * D_ATT], preferred_element_type=F32)
        if idx < 2:
            z = head_norm(z, qkg_ref[idx:idx + 1, :])
        elif idx == 4:
            z = _gelu_tanh(z)
        for p in range(N_GROUPS):
            o_ref[p] = z[:, p * LANES:(p + 1) * LANES].astype(BF16)


def _in_projection(x, mod, ln1_g, w_in_bf, gmat, qkg, tm):
    bsz, t, _ = x.shape
    grouped = jax.ShapeDtypeStruct((bsz, N_GROUPS, t, LANES), BF16)
    out_spec = pl.BlockSpec((None, N_GROUPS, tm, LANES), lambda b, i: (b, 0, i, 0))
    return pl.pallas_call(
        _inproj_kernel,
        grid=(bsz, t // tm),
        in_specs=[
            pl.BlockSpec((None, tm, D_MODEL), lambda b, i: (b, i, 0)),
            pl.BlockSpec((None, SUBLANES, D_MODEL), lambda b, i: (b, 0, 0)),
            _const_spec((1, D_MODEL)),
            _const_spec((D_MODEL, D_IN)),
            _const_spec((MXU_WIDTH, MXU_WIDTH)),
            _const_spec((SUBLANES, D_ATT)),
        ],
        out_specs=[out_spec] * 5,
        out_shape=[grouped] * 5,
        compiler_params=pltpu.CompilerParams(
            dimension_semantics=("parallel", "parallel"), vmem_limit_bytes=VMEM_LIMIT),
        name="in_projection",
    )(x, mod, ln1_g, w_in_bf, gmat, qkg)


def _attn_kernel(q_ref, k_ref, v_ref, bias_ref, o_ref, *, rows):
    lane = lax.broadcasted_iota(jnp.int32, (GRID_W, LANES), 1)
    low = lane < HEAD_DIM

    def scores(r):
        rs = jnp.clip(r - NA_ROWS // 2, 0, rows - NA_ROWS)
        q0 = pl.multiple_of(r * GRID_W, GRID_W)
        k0 = pl.multiple_of(rs * GRID_W, GRID_W)
        q2 = q_ref[pl.ds(q0, GRID_W), :].astype(F32)
        qq = jnp.concatenate([jnp.where(low, q2, 0.0), jnp.where(low, 0.0, q2)], axis=0).astype(BF16)
        k2 = k_ref[pl.ds(k0, WIN_KEYS), :]
        s = lax.dot_general(qq, k2, (((1,), (1,)), ((), ())), preferred_element_type=F32)
        return s + bias_ref[r - rs], q0, k0

    def softmax(s):
        m = jnp.max(s, axis=-1, keepdims=True)
        return jnp.exp2(s - m).astype(BF16)

    ones = jnp.ones((WIN_KEYS, LANES), BF16)

    def weighted(e, q0, k0):
        v2 = jnp.concatenate([v_ref[pl.ds(k0, WIN_KEYS), :], ones], axis=1)
        pv = jnp.dot(e, v2, preferred_element_type=F32)
        pv = pv[:, :LANES] * (1.0 / pv[:, LANES:])
        out = jnp.where(low, pv[:GRID_W], pv[GRID_W:])
        o_ref[pl.ds(q0, GRID_W), :] = out.astype(BF16)

    def group_step(g, carry):
        staged = [scores(g * ROW_UNROLL + i) for i in range(ROW_UNROLL)]
        probs = [softmax(s) for s, _, _ in staged]
        for e, (_, q0, k0) in zip(probs, staged):
            weighted(e, q0, k0)
        return carry

    lax.fori_loop(0, rows // ROW_UNROLL, group_step, 0)


def _attention(q, k, v, bias):
    bsz, _, t, _ = q.shape
    rows = t // GRID_W
    assert rows % ROW_UNROLL == 0 and rows >= NA_ROWS
    seq_spec = pl.BlockSpec((None, None, t, LANES), lambda p, b: (b, p, 0, 0))
    return pl.pallas_call(
        functools.partial(_attn_kernel, rows=rows),
        grid=(N_GROUPS, bsz),
        in_specs=[seq_spec, seq_spec, seq_spec,
                  pl.BlockSpec((None, NA_ROWS, 2 * GRID_W, WIN_KEYS), lambda p, b: (p, 0, 0, 0))],
        out_specs=seq_spec,
        out_shape=jax.ShapeDtypeStruct(q.shape, BF16),
        compiler_params=pltpu.CompilerParams(
            dimension_semantics=("parallel", "parallel"), vmem_limit_bytes=VMEM_LIMIT),
        name="nbr_attention",
    )(q, k, v, bias)


def _attention_bias(rpb):
    cols = np.arange(GRID_W)
    col_start = np.clip(cols - NA_COLS // 2, 0, GRID_W - NA_COLS)
    in_win = (cols[None, :] >= col_start[:, None]) & (cols[None, :] < col_start[:, None] + NA_COLS)
    dc = cols[None, :] - cols[:, None] + (NA_COLS - 1)
    col_hot = (np.arange(2 * NA_COLS - 1)[:, None, None] == dc[None]) & in_win[None]
    win = np.arange(NA_ROWS)
    row_hot = np.arange(2 * NA_ROWS - 1)[:, None, None] == (win[None, :] - win[:, None] + NA_ROWS - 1)[None]
    pairs = rpb.astype(F32).reshape(N_GROUPS, 2, 2 * NA_ROWS - 1, 2 * NA_COLS - 1)
    tab = jnp.einsum("pard,roi,dcx->poacix", pairs, jnp.asarray(row_hot, F32), jnp.asarray(col_hot, F32),
                     precision=lax.Precision.HIGHEST)
    tab = jnp.where(jnp.asarray(in_win)[None, None, None, :, None, :], tab * LOG2_E, MASK_VALUE)
    return tab.reshape(N_GROUPS, NA_ROWS, 2 * GRID_W, WIN_KEYS)


def _lru_kernel(xl_ref, gl_ref, cw_ref, wf_ref, wb_ref, par_ref, o_ref,
                xp_s, xc_s, hf_s, qf_s, hb_s, qb_s, *, t, tc):
    seg = t // SUBLANES
    pitch = seg + LRU_SEG_PAD
    n_conv = t // tc
    conv_per_seg = seg // tc
    n_steps = seg // LRU_STEPS
    zeros8 = jnp.zeros((SUBLANES, LANES), F32)
    ones8 = jnp.ones((SUBLANES, LANES), F32)
    xp_s[0:SUBLANES, :] = zeros8
    xp_s[SUBLANES + t:, :] = zeros8
    xp_s[SUBLANES:SUBLANES + t, :] = xl_ref[...].astype(F32)

    def softplus(z):
        return jnp.maximum(z, 0.0) + jnp.log(1.0 + jnp.exp(-jnp.abs(z)))

    half_decay = [(-0.5 * LRU_C) * softplus(-par_ref[row:row + 1, :]) for row in (2, 5)]

    def gates(xc, w_ref, direction):
        row0 = 3 * direction
        g = jnp.dot(xc.astype(BF16), w_ref[...], preferred_element_type=F32)
        tr = jnp.tanh(g[:, :LANES] + par_ref[row0:row0 + 1, :])
        ti = jnp.tanh(g[:, LANES:] + par_ref[row0 + 1:row0 + 2, :])
        log_a = half_decay[direction] * tr + half_decay[direction]
        a = jnp.exp(log_a)
        quarter_beta_sq = (a * a + 1.0) * (-0.25 * jnp.tanh(log_a))
        half_beta = quarter_beta_sq * lax.rsqrt(jnp.maximum(quarter_beta_sq, TINY))
        u = half_beta * ((ti + 1.0) * xc)
        return a, u

    for c in range(n_conv):
        lo = SUBLANES + c * tc
        acc = cw_ref[1:2, :] * xp_s[lo:lo + tc, :] + cw_ref[4:5, :]
        for tap in (0, 2, 3):
            acc = acc + cw_ref[tap:tap + 1, :] * xp_s[lo + tap - 1:lo + tap - 1 + tc, :]
        r0 = (c // conv_per_seg) * pitch + (c % conv_per_seg) * tc
        xc_s[r0:r0 + tc, :] = acc

    def tile(ref, k):
        return ref.at[pl.ds(k, SUBLANES, stride=pitch), :]

    def gather(k0):
        return jnp.concatenate([tile(xc_s, k0 + kk)[...] for kk in range(LRU_STEPS)], axis=0)

    def scan_step(c, carry):
        hf, qf, hb, qb = carry
        kf = c * LRU_STEPS
        kb = (n_steps - 1 - c) * LRU_STEPS
        af, uf = gates(gather(kf), wf_ref, 0)
        ab, ub = gates(gather(kb), wb_ref, 1)
        for kk in range(LRU_STEPS):
            rows = slice(kk * SUBLANES, (kk + 1) * SUBLANES)
            hf = af[rows] * hf + uf[rows]
            qf = af[rows] * qf
            tile(hf_s, kf + kk)[...] = hf
            tile(qf_s, kf + kk)[...] = qf
            kr = LRU_STEPS - 1 - kk
            rows = slice(kr * SUBLANES, (kr + 1) * SUBLANES)
            hb = ab[rows] * hb + ub[rows]
            qb = ab[rows] * qb
            tile(hb_s, kb + kr)[...] = hb
            tile(qb_s, kb + kr)[...] = qb
        return hf, qf, hb, qb

    hf, qf, hb, qb = lax.fori_loop(0, n_steps, scan_step, (zeros8, ones8, zeros8, ones8))

    cf = [jnp.zeros((1, LANES), F32)]
    for s in range(SUBLANES - 1):
        cf.append(qf[s:s + 1] * cf[s] + hf[s:s + 1])
    cb = [jnp.zeros((1, LANES), F32)]
    for s in range(SUBLANES - 1, 0, -1):
        cb.insert(0, qb[s:s + 1] * cb[0] + hb[s:s + 1])

    for c in range(n_conv):
        s = c // conv_per_seg
        r0 = s * pitch + (c % conv_per_seg) * tc
        h = (hf_s[r0:r0 + tc, :] + qf_s[r0:r0 + tc, :] * cf[s]) + (hb_s[r0:r0 + tc, :] + qb_s[r0:r0 + tc, :] * cb[s])
        o_ref[c * tc:(c + 1) * tc, :] = (h * gl_ref[c * tc:(c + 1) * tc, :].astype(F32)).astype(BF16)


def _rg_lru(xl, gl, cw, wf, wb, par, tc):
    bsz, _, t, _ = xl.shape
    seg = t // SUBLANES
    tc = min(tc, seg)
    assert seg % tc == 0 and seg % LRU_STEPS == 0
    pitched = SUBLANES * (seg + LRU_SEG_PAD)
    seq_spec = pl.BlockSpec((None, None, t, LANES), lambda b, g: (b, g, 0, 0))

    def group_spec(shape):
        return pl.BlockSpec((None,) + shape, lambda b, g: (g,) + (0,) * len(shape))

    return pl.pallas_call(
        functools.partial(_lru_kernel, t=t, tc=tc),
        grid=(bsz, N_GROUPS),
        in_specs=[seq_spec, seq_spec, group_spec((SUBLANES, LANES)), group_spec((LANES, 2 * LANES)),
                  group_spec((LANES, 2 * LANES)), group_spec((SUBLANES, LANES))],
        out_specs=seq_spec,
        out_shape=jax.ShapeDtypeStruct(xl.shape, BF16),
        scratch_shapes=[pltpu.VMEM((t + 2 * SUBLANES, LANES), F32)] + [pltpu.VMEM((pitched, LANES), F32)] * 5,
        compiler_params=pltpu.CompilerParams(
            dimension_semantics=("parallel", "parallel"), vmem_limit_bytes=VMEM_LIMIT),
        name="rg_lru",
    )(xl, gl, cw, wf, wb, par)


def _block_diag_pairs(w):
    w = w.reshape(N_GROUPS, 2, LRU_BLOCK, LRU_BLOCK)
    z = jnp.zeros((N_GROUPS, LRU_BLOCK, LRU_BLOCK), w.dtype)
    top = jnp.concatenate([w[:, 0], z], axis=-1)
    bot = jnp.concatenate([z, w[:, 1]], axis=-1)
    return jnp.concatenate([top, bot], axis=-2)


def _mlp_kernel(x_ref, xp_ref, xn_ref, a_ref, ap_ref, an_ref, l_ref, lp_ref, ln_ref, mod_ref, ln2_ref,
                wout_ref, wup_ref, cw_ref, wdn_ref, o_ref, al_s, x1_s, n2_s, acc_s, u_s, *, tm, t):
    i = pl.program_id(1)
    n_ext = tm + 2 * HALO
    for src, lane0 in (((ap_ref, a_ref, an_ref), 0), ((lp_ref, l_ref, ln_ref), D_ATT)):
        prev_ref, main_ref, next_ref = src
        for p in range(N_GROUPS):
            lanes = slice(lane0 + p * LANES, lane0 + (p + 1) * LANES)
            al_s[0:HALO, lanes] = prev_ref[p]
            al_s[HALO:HALO + tm, lanes] = main_ref[p]
            al_s[HALO + tm:, lanes] = next_ref[p]
    x1_s[0:HALO, :] = xp_ref[...]
    x1_s[HALO:HALO + tm, :] = x_ref[...]
    x1_s[HALO + tm:, :] = xn_ref[...]

    rows_per_block = n_ext // PROLOGUE_BLOCKS
    assert rows_per_block * PROLOGUE_BLOCKS == n_ext and rows_per_block % HALO == 0
    for blk in range(PROLOGUE_BLOCKS):
        r0, r1 = blk * rows_per_block, (blk + 1) * rows_per_block
        mix = jnp.dot(al_s[r0:r1, :], wout_ref[...], preferred_element_type=F32)
        x1 = x1_s[r0:r1, :] + mod_ref[2:3, :] * mix
        x1_s[r0:r1, :] = x1
        ms = jnp.mean(x1 * x1, axis=-1, keepdims=True)
        n2 = x1 * lax.rsqrt(ms + NORM_EPS) * ln2_ref[...] * (1.0 + mod_ref[4:5, :]) + mod_ref[3:4, :]
        tok = i * tm - HALO + r0 + lax.broadcasted_iota(jnp.int32, (r1 - r0, 1), 0)
        n2_s[r0:r1, :] = jnp.where((tok >= 0) & (tok < t), n2, 0.0).astype(BF16)

    def up(c, slot):
        c0, c1 = FF_CHUNKS[c]
        n2 = n2_s[...]
        u_s[slot, 0, :, 0:c1 - c0] = jnp.dot(n2, wup_ref[:, c0:c1], preferred_element_type=F32)
        u_s[slot, 1, :, 0:c1 - c0] = jnp.dot(n2, wup_ref[:, D_FF + c0:D_FF + c1], preferred_element_type=F32)

    def conv(slot, half, c0, c1):
        w = c1 - c0
        cw = cw_ref[:, half * D_FF + c0:half * D_FF + c1]
        c = cw[1:2, :] * u_s[slot, half, HALO:HALO + tm, 0:w] + cw[3:4, :]
        c = c + cw[0:1, :] * u_s[slot, half, HALO - 1:HALO - 1 + tm, 0:w]
        return c + cw[2:3, :] * u_s[slot, half, HALO + 1:HALO + 1 + tm, 0:w]

    def down(c, slot):
        c0, c1 = FF_CHUNKS[c]
        ug = conv(slot, 0, c0, c1)
        uv = conv(slot, 1, c0, c1)
        act = (ug * (1.0 + _gelu_inner_tanh(ug)) * uv).astype(BF16)
        d = jnp.dot(act, wdn_ref[c0:c1, :], preferred_element_type=F32)
        acc_s[...] = d if c == 0 else acc_s[...] + d

    up(0, 0)
    for c in range(len(FF_CHUNKS)):
        if c + 1 < len(FF_CHUNKS):
            up(c + 1, (c + 1) % 2)
        down(c, c % 2)
    o_ref[...] = x1_s[HALO:HALO + tm, :] + mod_ref[5:6, :] * acc_s[...]


def _out_mlp(x, att, lru, mod, ln2_g, w_out_bf, w_up_bf, ffn_cw, w_dn_bf, tm):
    bsz, t, _ = x.shape
    per_halo = tm // HALO
    n_halo = t // HALO

    def main_map(b, i):
        return (b, i, 0)

    def prev_map(b, i):
        return (b, jnp.maximum(i * per_halo - 1, 0), 0)

    def next_map(b, i):
        return (b, jnp.minimum((i + 1) * per_halo, n_halo - 1), 0)

    def grouped(rows, imap):
        return pl.BlockSpec((None, N_GROUPS, rows, LANES), lambda b, i: (imap(b, i)[0], 0, imap(b, i)[1], 0))

    n_ext = tm + 2 * HALO
    return pl.pallas_call(
        functools.partial(_mlp_kernel, tm=tm, t=t),
        grid=(bsz, t // tm),
        in_specs=[
            pl.BlockSpec((None, tm, D_MODEL), main_map),
            pl.BlockSpec((None, HALO, D_MODEL), prev_map),
            pl.BlockSpec((None, HALO, D_MODEL), next_map),
            grouped(tm, main_map), grouped(HALO, prev_map), grouped(HALO, next_map),
            grouped(tm, main_map), grouped(HALO, prev_map), grouped(HALO, next_map),
            pl.BlockSpec((None, SUBLANES, D_MODEL), lambda b, i: (b, 0, 0)),
            _const_spec((1, D_MODEL)),
            _const_spec((D_MODEL, D_MODEL)),
            _const_spec((D_MODEL, 2 * D_FF)),
            _const_spec((SUBLANES, 2 * D_FF)),
            _const_spec((D_FF, D_MODEL)),
        ],
        out_specs=pl.BlockSpec((None, tm, D_MODEL), main_map),
        out_shape=jax.ShapeDtypeStruct(x.shape, F32),
        scratch_shapes=[pltpu.VMEM((n_ext, D_MODEL), BF16),
                        pltpu.VMEM((n_ext, D_MODEL), F32),
                        pltpu.VMEM((n_ext, D_MODEL), BF16),
                        pltpu.VMEM((tm, D_MODEL), F32),
                        pltpu.VMEM((2, 2, n_ext, FF_CHUNK_MAX), F32)],
        compiler_params=pltpu.CompilerParams(
            dimension_semantics=("parallel", "parallel"), vmem_limit_bytes=VMEM_LIMIT),
        name="out_mlp",
    )(x, x, x, att, att, att, lru, lru, lru, mod, ln2_g, w_out_bf, w_up_bf, ffn_cw, w_dn_bf)


def _pad_rows(a, rows):
    return jnp.concatenate([a, jnp.zeros((rows - a.shape[0],) + a.shape[1:], a.dtype)], axis=0)


def _prepare_params(ln1_g, ln2_g, w_in, q_norm_g, k_norm_g, rpb, lru_conv_w, lru_conv_b,
                    w_r_f, b_r_f, w_i_f, b_i_f, lam_f, w_r_b, b_r_b, w_i_b, b_i_b, lam_b,
                    w_out, w_up, ffn_conv_w, ffn_conv_b, w_down):
    head = jnp.arange(MXU_WIDTH) // HEAD_DIM
    gmat = (head[:, None] == head[None, :]).astype(BF16)
    scale = HEAD_DIM ** -0.5 * LOG2_E
    qkg = _pad_rows(jnp.stack([jnp.tile(q_norm_g.astype(F32), ATT_HEADS) * scale,
                               jnp.tile(k_norm_g.astype(F32), ATT_HEADS)]), SUBLANES)
    lru_cw = _pad_rows(jnp.concatenate([lru_conv_w, lru_conv_b[None]], axis=0), SUBLANES)
    lru_cw = jnp.transpose(lru_cw.reshape(SUBLANES, N_GROUPS, LANES), (1, 0, 2))
    wf = (0.5 * jnp.concatenate([_block_diag_pairs(w_r_f), _block_diag_pairs(w_i_f)], axis=-1)).astype(BF16)
    wb = (0.5 * jnp.concatenate([_block_diag_pairs(w_r_b), _block_diag_pairs(w_i_b)], axis=-1)).astype(BF16)
    par = _pad_rows(jnp.stack([0.5 * b_r_f, 0.5 * b_i_f, lam_f, 0.5 * b_r_b, 0.5 * b_i_b, lam_b]), SUBLANES)
    par = jnp.transpose(par.reshape(SUBLANES, N_GROUPS, LANES), (1, 0, 2))
    ffn_cw = _pad_rows(jnp.concatenate([ffn_conv_w, ffn_conv_b[None]], axis=0), SUBLANES)
    ffn_cw = ffn_cw * jnp.where(jnp.arange(2 * D_FF) < D_FF, 1.0, 0.5)[None, :]
    return dict(
        ln1_g=ln1_g.reshape(1, D_MODEL), ln2_g=ln2_g.reshape(1, D_MODEL),
        w_in=w_in.astype(BF16), gmat=gmat, qkg=qkg, bias=_attention_bias(rpb),
        lru_cw=lru_cw, wf=wf, wb=wb, par=par,
        w_out=w_out.astype(BF16), w_up=w_up.astype(BF16), ffn_cw=ffn_cw, w_down=w_down.astype(BF16))


def _encoder_layer(x, mod, p, tm, tc):
    q, k, v, xl, gl = _in_projection(x, mod, p["ln1_g"], p["w_in"], p["gmat"], p["qkg"], tm)
    att = _attention(q, k, v, p["bias"])
    lru = _rg_lru(xl, gl, p["lru_cw"], p["wf"], p["wb"], p["par"], tc)
    return _out_mlp(x, att, lru, mod, p["ln2_g"], p["w_out"], p["w_up"], p["ffn_cw"], p["w_down"], tm)


def _forward(x_prompt, x_sample, c_prompt, c_sample, w_ada, b_ada, layer_params, tm=512, tc=512):
    nb_p, nb_s = c_prompt.shape[0], c_sample.shape[0]
    depth = w_ada.shape[0]
    for layer in range(depth):
        c_all = jnp.concatenate([c_prompt, c_sample], axis=0)
        c_all = _pad_rows(c_all, -(-c_all.shape[0] // SUBLANES) * SUBLANES)
        mod = _modulation(c_all, w_ada[layer], b_ada[layer])
        mod = mod.reshape(mod.shape[0], N_MOD, D_MODEL)
        mod = jnp.concatenate([mod, jnp.zeros((mod.shape[0], SUBLANES - N_MOD, D_MODEL), F32)], axis=1)
        p = _prepare_params(*[w[layer] for w in layer_params])
        x_prompt = _encoder_layer(x_prompt, mod[:nb_p], p, tm, tc)
        x_sample = _encoder_layer(x_sample, mod[nb_p:nb_p + nb_s], p, tm, tc)
    return x_prompt, x_sample


def kernel(x_prompt, x_sample, c_prompt, c_sample, ln1_g, ln2_g, w_ada, b_ada, w_in, q_norm_g, k_norm_g, rpb,
           lru_conv_w, lru_conv_b, w_r_f, b_r_f, w_i_f, b_i_f, lam_f, w_r_b, b_r_b, w_i_b, b_i_b, lam_b,
           w_out, w_up, ffn_conv_w, ffn_conv_b, w_down):
    layer_params = (ln1_g, ln2_g, w_in, q_norm_g, k_norm_g, rpb, lru_conv_w, lru_conv_b,
                    w_r_f, b_r_f, w_i_f, b_i_f, lam_f, w_r_b, b_r_b, w_i_b, b_i_b, lam_b,
                    w_out, w_up, ffn_conv_w, ffn_conv_b, w_down)
    return _forward(x_prompt, x_sample, c_prompt, c_sample, w_ada, b_ada, layer_params)
```

```python
import functools
import math

import numpy as np

import jax
import jax.numpy as jnp
from jax import lax
from jax.experimental import pallas as pl
from jax.experimental.pallas import tpu as pltpu

D_MODEL = 1024
GRID_W = 64
ATT_HEADS = 8
HEAD_DIM = 64
D_ATT = ATT_HEADS * HEAD_DIM
D_LRU = 512
LRU_HEADS = 8
LRU_BLOCK = D_LRU // LRU_HEADS
NA_ROWS = 8
NA_COLS = 16
LRU_CONV = 4
LRU_C = 8.0
FFN_CONV = 3
D_FF = 2816
D_IN = 3 * D_ATT + 2 * D_LRU
N_MOD = 6
NORM_EPS = 1e-6

LANES = 128
SUBLANES = 8
MXU_WIDTH = 256
N_GROUPS = D_ATT // LANES
WIN_KEYS = NA_ROWS * GRID_W
FF_CHUNKS = ((0, 7 * MXU_WIDTH), (7 * MXU_WIDTH, D_FF))
FF_CHUNK_MAX = max(c1 - c0 for c0, c1 in FF_CHUNKS)
HALO = 16
PROLOGUE_BLOCKS = 2
ROW_UNROLL = 64
LRU_STEPS = 256
LRU_SEG_PAD = 8
MASK_VALUE = -1e30
LOG2_E = math.log2(math.e)
TINY = 1e-30
VMEM_LIMIT = 56 * 1024 * 1024

BF16 = jnp.bfloat16
F32 = jnp.float32


def _gelu_tanh(x):
    return 0.5 * x * (1.0 + jnp.tanh(math.sqrt(2.0 / math.pi) * (x + 0.044715 * (x * x * x))))


def _gelu_inner_tanh(x):
    c = math.sqrt(2.0 / math.pi)
    return jnp.tanh(x * (c + (0.044715 * c) * (x * x)))


def _sigmoid(x):
    return 0.5 * jnp.tanh(0.5 * x) + 0.5


def _const_spec(shape):
    zeros = (0,) * len(shape)
    return pl.BlockSpec(shape, lambda *_: zeros, pipeline_mode=pl.Buffered(1))


def _mod_kernel(c_ref, w_ref, b_ref, o_ref):
    c = c_ref[...]
    s = (c * _sigmoid(c)).astype(BF16)
    o_ref[...] = jnp.dot(s, w_ref[...].astype(BF16), preferred_element_type=F32) + b_ref[...]


def _modulation(c_all, w_ada, b_ada):
    bp = c_all.shape[0]
    return pl.pallas_call(
        _mod_kernel,
        grid=(N_MOD,),
        in_specs=[
            pl.BlockSpec((bp, D_MODEL), lambda j: (0, 0)),
            pl.BlockSpec((D_MODEL, D_MODEL), lambda j: (0, j)),
            pl.BlockSpec((1, D_MODEL), lambda j: (0, j)),
        ],
        out_specs=pl.BlockSpec((bp, D_MODEL), lambda j: (0, j)),
        out_shape=jax.ShapeDtypeStruct((bp, N_MOD * D_MODEL), F32),
        compiler_params=pltpu.CompilerParams(dimension_semantics=("arbitrary",), vmem_limit_bytes=VMEM_LIMIT),
        name="modulation",
    )(c_all, w_ada, b_ada.reshape(1, -1))


def _inproj_kernel(x_ref, mod_ref, ln_ref, w_ref, gmat_ref, qkg_ref,
                   q_ref, k_ref, v_ref, xl_ref, gl_ref):
    x = x_ref[...]
    ms = jnp.mean(x * x, axis=-1, keepdims=True)
    y = x * lax.rsqrt(ms + NORM_EPS) * ln_ref[...]
    n = (y * (1.0 + mod_ref[1:2, :]) + mod_ref[0:1, :]).astype(BF16)

    def head_norm(z, gain):
        zz = (z * z).astype(BF16)
        width = gmat_ref.shape[0]
        ss = jnp.concatenate(
            [jnp.dot(zz[:, c:c + width], gmat_ref[...], preferred_element_type=F32) for c in range(0, D_ATT, width)],
            axis=1)
        return z * lax.rsqrt(ss * (1.0 / HEAD_DIM) + NORM_EPS) * gain

    outs = (q_ref, k_ref, v_ref, xl_ref, gl_ref)
    for idx, o_ref in enumerate(outs):
        z = jnp.dot(n, w_ref[:, idx * D_ATT:(idx + 1) * D_ATT], preferred_element_type=F32)
        if idx < 2:
            z = head_norm(z, qkg_ref[idx:idx + 1, :])
        elif idx == 4:
            z = _gelu_tanh(z)
        for p in range(N_GROUPS):
            o_ref[p] = z[:, p * LANES:(p + 1) * LANES].astype(BF16)


def _in_projection(x, mod, ln1_g, w_in_bf, gmat, qkg, tm):
    bsz, t, _ = x.shape
    grouped = jax.ShapeDtypeStruct((bsz, N_GROUPS, t, LANES), BF16)
    out_spec = pl.BlockSpec((None, N_GROUPS, tm, LANES), lambda b, i: (b, 0, i, 0))
    return pl.pallas_call(
        _inproj_kernel,
        grid=(bsz, t // tm),
        in_specs=[
            pl.BlockSpec((None, tm, D_MODEL), lambda b, i: (b, i, 0)),
            pl.BlockSpec((None, SUBLANES, D_MODEL), lambda b, i: (b, 0, 0)),
            _const_spec((1, D_MODEL)),
            _const_spec((D_MODEL, D_IN)),
            _const_spec((MXU_WIDTH, MXU_WIDTH)),
            _const_spec((SUBLANES, D_ATT)),
        ],
        out_specs=[out_spec] * 5,
        out_shape=[grouped] * 5,
        compiler_params=pltpu.CompilerParams(
            dimension_semantics=("parallel", "parallel"), vmem_limit_bytes=VMEM_LIMIT),
        name="in_projection",
    )(x, mod, ln1_g, w_in_bf, gmat, qkg)


def _attn_kernel(q_ref, k_ref, v_ref, bias_ref, o_ref, *, rows):
    lane = lax.broadcasted_iota(jnp.int32, (GRID_W, LANES), 1)
    low = lane < HEAD_DIM

    def scores(r):
        rs = jnp.clip(r - NA_ROWS // 2, 0, rows - NA_ROWS)
        q0 = pl.multiple_of(r * GRID_W, GRID_W)
        k0 = pl.multiple_of(rs * GRID_W, GRID_W)
        q2 = q_ref[pl.ds(q0, GRID_W), :].astype(F32)
        qq = jnp.concatenate([jnp.where(low, q2, 0.0), jnp.where(low, 0.0, q2)], axis=0).astype(BF16)
        k2 = k_ref[pl.ds(k0, WIN_KEYS), :]
        s = lax.dot_general(qq, k2, (((1,), (1,)), ((), ())), preferred_element_type=F32)
        return s + bias_ref[r - rs], q0, k0

    def softmax(s):
        m = jnp.max(s, axis=-1, keepdims=True)
        return jnp.exp2(s - m).astype(BF16)

    ones = jnp.ones((WIN_KEYS, LANES), BF16)

    def weighted(e, q0, k0):
        v2 = jnp.concatenate([v_ref[pl.ds(k0, WIN_KEYS), :], ones], axis=1)
        pv = jnp.dot(e, v2, preferred_element_type=F32)
        pv = pv[:, :LANES] * (1.0 / pv[:, LANES:])
        out = jnp.where(low, pv[:GRID_W], pv[GRID_W:])
        o_ref[pl.ds(q0, GRID_W), :] = out.astype(BF16)

    def group_step(g, carry):
        staged = [scores(g * ROW_UNROLL + i) for i in range(ROW_UNROLL)]
        probs = [softmax(s) for s, _, _ in staged]
        for e, (_, q0, k0) in zip(probs, staged):
            weighted(e, q0, k0)
        return carry

    lax.fori_loop(0, rows // ROW_UNROLL, group_step, 0)


def _attention(q, k, v, bias):
    bsz, _, t, _ = q.shape
    rows = t // GRID_W
    assert rows % ROW_UNROLL == 0 and rows >= NA_ROWS
    seq_spec = pl.BlockSpec((None, None, t, LANES), lambda p, b: (b, p, 0, 0))
    return pl.pallas_call(
        functools.partial(_attn_kernel, rows=rows),
        grid=(N_GROUPS, bsz),
        in_specs=[seq_spec, seq_spec, seq_spec,
                  pl.BlockSpec((None, NA_ROWS, 2 * GRID_W, WIN_KEYS), lambda p, b: (p, 0, 0, 0))],
        out_specs=seq_spec,
        out_shape=jax.ShapeDtypeStruct(q.shape, BF16),
        compiler_params=pltpu.CompilerParams(
            dimension_semantics=("parallel", "parallel"), vmem_limit_bytes=VMEM_LIMIT),
        name="nbr_attention",
    )(q, k, v, bias)


def _attention_bias(rpb):
    cols = np.arange(GRID_W)
    col_start = np.clip(cols - NA_COLS // 2, 0, GRID_W - NA_COLS)
    in_win = (cols[None, :] >= col_start[:, None]) & (cols[None, :] < col_start[:, None] + NA_COLS)
    dc = cols[None, :] - cols[:, None] + (NA_COLS - 1)
    onehot = (np.arange(2 * NA_COLS - 1)[:, None, None] == dc[None]) & in_win[None]
    tab = jnp.einsum("hrd,dcx->hrcx", rpb.astype(F32), jnp.asarray(onehot, F32),
                     precision=lax.Precision.HIGHEST)
    tab = jnp.where(jnp.asarray(in_win)[None, None], tab * LOG2_E, MASK_VALUE)
    b = jnp.stack([tab[:, NA_ROWS - 1 - o:2 * NA_ROWS - 1 - o] for o in range(NA_ROWS)], axis=1)
    b = jnp.transpose(b, (0, 1, 3, 2, 4)).reshape(ATT_HEADS, NA_ROWS, GRID_W, WIN_KEYS)
    b = b.reshape(N_GROUPS, 2, NA_ROWS, GRID_W, WIN_KEYS)
    return jnp.transpose(b, (0, 2, 1, 3, 4)).reshape(N_GROUPS, NA_ROWS, 2 * GRID_W, WIN_KEYS)


def _lru_kernel(xl_ref, gl_ref, cw_ref, wf_ref, wb_ref, par_ref, o_ref,
                xp_s, xc_s, hf_s, qf_s, hb_s, qb_s, *, t, tc):
    seg = t // SUBLANES
    pitch = seg + LRU_SEG_PAD
    n_conv = t // tc
    conv_per_seg = seg // tc
    n_steps = seg // LRU_STEPS
    zeros8 = jnp.zeros((SUBLANES, LANES), F32)
    ones8 = jnp.ones((SUBLANES, LANES), F32)
    xp_s[0:SUBLANES, :] = zeros8
    xp_s[SUBLANES + t:, :] = zeros8
    xp_s[SUBLANES:SUBLANES + t, :] = xl_ref[...].astype(F32)

    def softplus(z):
        return jnp.maximum(z, 0.0) + jnp.log(1.0 + jnp.exp(-jnp.abs(z)))

    half_decay = [(-0.5 * LRU_C) * softplus(-par_ref[row:row + 1, :]) for row in (2, 5)]

    def gates(xc, w_ref, direction):
        row0 = 3 * direction
        g = jnp.dot(xc.astype(BF16), w_ref[...], preferred_element_type=F32)
        tr = jnp.tanh(g[:, :LANES] + par_ref[row0:row0 + 1, :])
        ti = jnp.tanh(g[:, LANES:] + par_ref[row0 + 1:row0 + 2, :])
        log_a = half_decay[direction] * tr + half_decay[direction]
        a = jnp.exp(log_a)
        quarter_beta_sq = (a * a + 1.0) * (-0.25 * jnp.tanh(log_a))
        half_beta = quarter_beta_sq * lax.rsqrt(jnp.maximum(quarter_beta_sq, TINY))
        u = half_beta * ((ti + 1.0) * xc)
        return a, u

    for c in range(n_conv):
        lo = SUBLANES + c * tc
        acc = cw_ref[1:2, :] * xp_s[lo:lo + tc, :] + cw_ref[4:5, :]
        for tap in (0, 2, 3):
            acc = acc + cw_ref[tap:tap + 1, :] * xp_s[lo + tap - 1:lo + tap - 1 + tc, :]
        r0 = (c // conv_per_seg) * pitch + (c % conv_per_seg) * tc
        xc_s[r0:r0 + tc, :] = acc

    def tile(ref, k):
        return ref.at[pl.ds(k, SUBLANES, stride=pitch), :]

    def gather(k0):
        return jnp.concatenate([tile(xc_s, k0 + kk)[...] for kk in range(LRU_STEPS)], axis=0)

    def scan_step(c, carry):
        hf, qf, hb, qb = carry
        kf = c * LRU_STEPS
        kb = (n_steps - 1 - c) * LRU_STEPS
        af, uf = gates(gather(kf), wf_ref, 0)
        ab, ub = gates(gather(kb), wb_ref, 1)
        for kk in range(LRU_STEPS):
            rows = slice(kk * SUBLANES, (kk + 1) * SUBLANES)
            hf = af[rows] * hf + uf[rows]
            qf = af[rows] * qf
            tile(hf_s, kf + kk)[...] = hf
            tile(qf_s, kf + kk)[...] = qf
            kr = LRU_STEPS - 1 - kk
            rows = slice(kr * SUBLANES, (kr + 1) * SUBLANES)
            hb = ab[rows] * hb + ub[rows]
            qb = ab[rows] * qb
            tile(hb_s, kb + kr)[...] = hb
            tile(qb_s, kb + kr)[...] = qb
        return hf, qf, hb, qb

    hf, qf, hb, qb = lax.fori_loop(0, n_steps, scan_step, (zeros8, ones8, zeros8, ones8))

    cf = [jnp.zeros((1, LANES), F32)]
    for s in range(SUBLANES - 1):
        cf.append(qf[s:s + 1] * cf[s] + hf[s:s + 1])
    cb = [jnp.zeros((1, LANES), F32)]
    for s in range(SUBLANES - 1, 0, -1):
        cb.insert(0, qb[s:s + 1] * cb[0] + hb[s:s + 1])

    for c in range(n_conv):
        s = c // conv_per_seg
        r0 = s * pitch + (c % conv_per_seg) * tc
        h = (hf_s[r0:r0 + tc, :] + qf_s[r0:r0 + tc, :] * cf[s]) + (hb_s[r0:r0 + tc, :] + qb_s[r0:r0 + tc, :] * cb[s])
        o_ref[c * tc:(c + 1) * tc, :] = (h * gl_ref[c * tc:(c + 1) * tc, :].astype(F32)).astype(BF16)


def _rg_lru(xl, gl, cw, wf, wb, par, tc):
    bsz, _, t, _ = xl.shape
    seg = t // SUBLANES
    tc = min(tc, seg)
    assert seg % tc == 0 and seg % LRU_STEPS == 0
    pitched = SUBLANES * (seg + LRU_SEG_PAD)
    seq_spec = pl.BlockSpec((None, None, t, LANES), lambda b, g: (b, g, 0, 0))

    def group_spec(shape):
        return pl.BlockSpec((None,) + shape, lambda b, g: (g,) + (0,) * len(shape))

    return pl.pallas_call(
        functools.partial(_lru_kernel, t=t, tc=tc),
        grid=(bsz, N_GROUPS),
        in_specs=[seq_spec, seq_spec, group_spec((SUBLANES, LANES)), group_spec((LANES, 2 * LANES)),
                  group_spec((LANES, 2 * LANES)), group_spec((SUBLANES, LANES))],
        out_specs=seq_spec,
        out_shape=jax.ShapeDtypeStruct(xl.shape, BF16),
        scratch_shapes=[pltpu.VMEM((t + 2 * SUBLANES, LANES), F32)] + [pltpu.VMEM((pitched, LANES), F32)] * 5,
        compiler_params=pltpu.CompilerParams(
            dimension_semantics=("parallel", "parallel"), vmem_limit_bytes=VMEM_LIMIT),
        name="rg_lru",
    )(xl, gl, cw, wf, wb, par)


def _block_diag_pairs(w):
    w = w.reshape(N_GROUPS, 2, LRU_BLOCK, LRU_BLOCK)
    z = jnp.zeros((N_GROUPS, LRU_BLOCK, LRU_BLOCK), w.dtype)
    top = jnp.concatenate([w[:, 0], z], axis=-1)
    bot = jnp.concatenate([z, w[:, 1]], axis=-1)
    return jnp.concatenate([top, bot], axis=-2)


def _mlp_kernel(x_ref, xp_ref, xn_ref, a_ref, ap_ref, an_ref, l_ref, lp_ref, ln_ref, mod_ref, ln2_ref,
                wout_ref, wup_ref, cw_ref, wdn_ref, o_ref, al_s, x1_s, n2_s, acc_s, u_s, *, tm, t):
    i = pl.program_id(1)
    n_ext = tm + 2 * HALO
    for src, lane0 in (((ap_ref, a_ref, an_ref), 0), ((lp_ref, l_ref, ln_ref), D_ATT)):
        prev_ref, main_ref, next_ref = src
        for p in range(N_GROUPS):
            lanes = slice(lane0 + p * LANES, lane0 + (p + 1) * LANES)
            al_s[0:HALO, lanes] = prev_ref[p]
            al_s[HALO:HALO + tm, lanes] = main_ref[p]
            al_s[HALO + tm:, lanes] = next_ref[p]
    x1_s[0:HALO, :] = xp_ref[...]
    x1_s[HALO:HALO + tm, :] = x_ref[...]
    x1_s[HALO + tm:, :] = xn_ref[...]

    rows_per_block = n_ext // PROLOGUE_BLOCKS
    assert rows_per_block * PROLOGUE_BLOCKS == n_ext and rows_per_block % HALO == 0
    for blk in range(PROLOGUE_BLOCKS):
        r0, r1 = blk * rows_per_block, (blk + 1) * rows_per_block
        mix = jnp.dot(al_s[r0:r1, :], wout_ref[...], preferred_element_type=F32)
        x1 = x1_s[r0:r1, :] + mod_ref[2:3, :] * mix
        x1_s[r0:r1, :] = x1
        ms = jnp.mean(x1 * x1, axis=-1, keepdims=True)
        n2 = x1 * lax.rsqrt(ms + NORM_EPS) * ln2_ref[...] * (1.0 + mod_ref[4:5, :]) + mod_ref[3:4, :]
        tok = i * tm - HALO + r0 + lax.broadcasted_iota(jnp.int32, (r1 - r0, 1), 0)
        n2_s[r0:r1, :] = jnp.where((tok >= 0) & (tok < t), n2, 0.0).astype(BF16)

    def up(c, slot):
        c0, c1 = FF_CHUNKS[c]
        for blk in range(PROLOGUE_BLOCKS):
            r0, r1 = blk * rows_per_block, (blk + 1) * rows_per_block
            n2 = n2_s[r0:r1, :]
            for half in range(2):
                u_s[slot, half, r0:r1, 0:c1 - c0] = jnp.dot(
                    n2, wup_ref[:, half * D_FF + c0:half * D_FF + c1], preferred_element_type=F32)

    def conv(slot, half, c0, c1):
        w = c1 - c0
        cw = cw_ref[:, half * D_FF + c0:half * D_FF + c1]
        c = cw[1:2, :] * u_s[slot, half, HALO:HALO + tm, 0:w] + cw[3:4, :]
        c = c + cw[0:1, :] * u_s[slot, half, HALO - 1:HALO - 1 + tm, 0:w]
        return c + cw[2:3, :] * u_s[slot, half, HALO + 1:HALO + 1 + tm, 0:w]

    def down(c, slot):
        c0, c1 = FF_CHUNKS[c]
        ug = conv(slot, 0, c0, c1)
        uv = conv(slot, 1, c0, c1)
        act = (ug * (1.0 + _gelu_inner_tanh(ug)) * uv).astype(BF16)
        d = jnp.dot(act, wdn_ref[c0:c1, :], preferred_element_type=F32)
        acc_s[...] = d if c == 0 else acc_s[...] + d

    up(0, 0)
    for c in range(len(FF_CHUNKS)):
        if c + 1 < len(FF_CHUNKS):
            up(c + 1, (c + 1) % 2)
        down(c, c % 2)
    o_ref[...] = x1_s[HALO:HALO + tm, :] + mod_ref[5:6, :] * acc_s[...]


def _out_mlp(x, att, lru, mod, ln2_g, w_out_bf, w_up_bf, ffn_cw, w_dn_bf, tm):
    bsz, t, _ = x.shape
    per_halo = tm // HALO
    n_halo = t // HALO

    def main_map(b, i):
        return (b, i, 0)

    def prev_map(b, i):
        return (b, jnp.maximum(i * per_halo - 1, 0), 0)

    def next_map(b, i):
        return (b, jnp.minimum((i + 1) * per_halo, n_halo - 1), 0)

    def grouped(rows, imap):
        return pl.BlockSpec((None, N_GROUPS, rows, LANES), lambda b, i: (imap(b, i)[0], 0, imap(b, i)[1], 0))

    n_ext = tm + 2 * HALO
    return pl.pallas_call(
        functools.partial(_mlp_kernel, tm=tm, t=t),
        grid=(bsz, t // tm),
        in_specs=[
            pl.BlockSpec((None, tm, D_MODEL), main_map),
            pl.BlockSpec((None, HALO, D_MODEL), prev_map),
            pl.BlockSpec((None, HALO, D_MODEL), next_map),
            grouped(tm, main_map), grouped(HALO, prev_map), grouped(HALO, next_map),
            grouped(tm, main_map), grouped(HALO, prev_map), grouped(HALO, next_map),
            pl.BlockSpec((None, SUBLANES, D_MODEL), lambda b, i: (b, 0, 0)),
            _const_spec((1, D_MODEL)),
            _const_spec((D_MODEL, D_MODEL)),
            _const_spec((D_MODEL, 2 * D_FF)),
            _const_spec((SUBLANES, 2 * D_FF)),
            _const_spec((D_FF, D_MODEL)),
        ],
        out_specs=pl.BlockSpec((None, tm, D_MODEL), main_map),
        out_shape=jax.ShapeDtypeStruct(x.shape, F32),
        scratch_shapes=[pltpu.VMEM((n_ext, D_MODEL), BF16),
                        pltpu.VMEM((n_ext, D_MODEL), F32),
                        pltpu.VMEM((n_ext, D_MODEL), BF16),
                        pltpu.VMEM((tm, D_MODEL), F32),
                        pltpu.VMEM((2, 2, n_ext, FF_CHUNK_MAX), F32)],
        compiler_params=pltpu.CompilerParams(
            dimension_semantics=("parallel", "parallel"), vmem_limit_bytes=VMEM_LIMIT),
        name="out_mlp",
    )(x, x, x, att, att, att, lru, lru, lru, mod, ln2_g, w_out_bf, w_up_bf, ffn_cw, w_dn_bf)


def _pad_rows(a, rows):
    return jnp.concatenate([a, jnp.zeros((rows - a.shape[0],) + a.shape[1:], a.dtype)], axis=0)


def _prepare_params(ln1_g, ln2_g, w_in, q_norm_g, k_norm_g, rpb, lru_conv_w, lru_conv_b,
                    w_r_f, b_r_f, w_i_f, b_i_f, lam_f, w_r_b, b_r_b, w_i_b, b_i_b, lam_b,
                    w_out, w_up, ffn_conv_w, ffn_conv_b, w_down):
    head = jnp.arange(MXU_WIDTH) // HEAD_DIM
    gmat = (head[:, None] == head[None, :]).astype(BF16)
    scale = HEAD_DIM ** -0.5 * LOG2_E
    qkg = _pad_rows(jnp.stack([jnp.tile(q_norm_g.astype(F32), ATT_HEADS) * scale,
                               jnp.tile(k_norm_g.astype(F32), ATT_HEADS)]), SUBLANES)
    lru_cw = _pad_rows(jnp.concatenate([lru_conv_w, lru_conv_b[None]], axis=0), SUBLANES)
    lru_cw = jnp.transpose(lru_cw.reshape(SUBLANES, N_GROUPS, LANES), (1, 0, 2))
    wf = (0.5 * jnp.concatenate([_block_diag_pairs(w_r_f), _block_diag_pairs(w_i_f)], axis=-1)).astype(BF16)
    wb = (0.5 * jnp.concatenate([_block_diag_pairs(w_r_b), _block_diag_pairs(w_i_b)], axis=-1)).astype(BF16)
    par = _pad_rows(jnp.stack([0.5 * b_r_f, 0.5 * b_i_f, lam_f, 0.5 * b_r_b, 0.5 * b_i_b, lam_b]), SUBLANES)
    par = jnp.transpose(par.reshape(SUBLANES, N_GROUPS, LANES), (1, 0, 2))
    ffn_cw = _pad_rows(jnp.concatenate([ffn_conv_w, ffn_conv_b[None]], axis=0), SUBLANES)
    ffn_cw = ffn_cw * jnp.where(jnp.arange(2 * D_FF) < D_FF, 1.0, 0.5)[None, :]
    return dict(
        ln1_g=ln1_g.reshape(1, D_MODEL), ln2_g=ln2_g.reshape(1, D_MODEL),
        w_in=w_in.astype(BF16), gmat=gmat, qkg=qkg, bias=_attention_bias(rpb),
        lru_cw=lru_cw, wf=wf, wb=wb, par=par,
        w_out=w_out.astype(BF16), w_up=w_up.astype(BF16), ffn_cw=ffn_cw, w_down=w_down.astype(BF16))


def _encoder_layer(x, mod, p, tm, tc):
    q, k, v, xl, gl = _in_projection(x, mod, p["ln1_g"], p["w_in"], p["gmat"], p["qkg"], tm)
    att = _attention(q, k, v, p["bias"])
    lru = _rg_lru(xl, gl, p["lru_cw"], p["wf"], p["wb"], p["par"], tc)
    return _out_mlp(x, att, lru, mod, p["ln2_g"], p["w_out"], p["w_up"], p["ffn_cw"], p["w_down"], tm)


def _forward(x_prompt, x_sample, c_prompt, c_sample, w_ada, b_ada, layer_params, tm=512, tc=512):
    nb_p, nb_s = c_prompt.shape[0], c_sample.shape[0]
    depth = w_ada.shape[0]
    for layer in range(depth):
        c_all = jnp.concatenate([c_prompt, c_sample], axis=0)
        c_all = _pad_rows(c_all, -(-c_all.shape[0] // SUBLANES) * SUBLANES)
        mod = _modulation(c_all, w_ada[layer], b_ada[layer])
        mod = mod.reshape(mod.shape[0], N_MOD, D_MODEL)
        mod = jnp.concatenate([mod, jnp.zeros((mod.shape[0], SUBLANES - N_MOD, D_MODEL), F32)], axis=1)
        p = _prepare_params(*[w[layer] for w in layer_params])
        x_prompt = _encoder_layer(x_prompt, mod[:nb_p], p, tm, tc)
        x_sample = _encoder_layer(x_sample, mod[nb_p:nb_p + nb_s], p, tm, tc)
    return x_prompt, x_sample


def kernel(x_prompt, x_sample, c_prompt, c_sample, ln1_g, ln2_g, w_ada, b_ada, w_in, q_norm_g, k_norm_g, rpb,
           lru_conv_w, lru_conv_b, w_r_f, b_r_f, w_i_f, b_i_f, lam_f, w_r_b, b_r_b, w_i_b, b_i_b, lam_b,
           w_out, w_up, ffn_conv_w, ffn_conv_b, w_down):
    layer_params = (ln1_g, ln2_g, w_in, q_norm_g, k_norm_g, rpb, lru_conv_w, lru_conv_b,
                    w_r_f, b_r_f, w_i_f, b_i_f, lam_f, w_r_b, b_r_b, w_i_b, b_i_b, lam_b,
                    w_out, w_up, ffn_conv_w, ffn_conv_b, w_down)
    return _forward(x_prompt, x_sample, c_prompt, c_sample, w_ada, b_ada, layer_params)
```
